```python
import math
import jax, jax.numpy as jnp
from jax import lax
import numpy as np

D_MODEL = 1024
BATCH = 32
SEQ = 2048
DEPTH = 2

D_CONV_BR = D_MODEL
CONV_A_WIDTH = 3
SSM_EXPAND = 2
D_SSM = SSM_EXPAND * D_MODEL
SSM_HEAD_DIM = 64
SSM_HEADS = D_SSM // SSM_HEAD_DIM
SSM_GROUPS = 4
HEADS_PER_GROUP = SSM_HEADS // SSM_GROUPS
SSM_STATE = 128
SSM_CONV_WIDTH = 4
SSM_CHUNK = 128
D_XBC = D_SSM + 2 * SSM_GROUPS * SSM_STATE

SPLIT_SIZES = (
    D_CONV_BR,
    D_CONV_BR,
    D_CONV_BR,
    D_CONV_BR,
    D_SSM,
    D_XBC,
    SSM_HEADS,
    D_MODEL,
    D_MODEL,
)
N_IN = sum(SPLIT_SIZES)

DEEPNORM_ALPHA = (2 * DEPTH) ** 0.25
DEEPNORM_BETA = (8 * DEPTH) ** -0.25
LN_EPS = 1e-5
RMS_EPS = 1e-5
DT_MIN = 1e-3
DT_MAX = 1e-1
A_INIT_MIN = 1.0
A_INIT_MAX = 16.0

kernel_name = "hybrid_shortconv_ssd_gated_merge_deepnorm"


def layer_norm(x, g, b):
    xf = x.astype(jnp.float32)
    mu = jnp.mean(xf, axis=-1, keepdims=True)
    var = jnp.mean(jnp.square(xf - mu), axis=-1, keepdims=True)
    return ((xf - mu) * lax.rsqrt(var + LN_EPS) * g + b).astype(x.dtype)


def causal_depthwise_conv(x, w):
    k = w.shape[0]
    return lax.conv_general_dilated(
        x, w[:, None, :].astype(x.dtype), window_strides=(1,), padding=[(k - 1, 0)],
        dimension_numbers=("NWC", "WIO", "NWC"), feature_group_count=x.shape[-1])


def gated_group_rmsnorm(y, z, g):
    h = (y * jax.nn.silu(z)).astype(jnp.float32)
    hg = h.reshape(*h.shape[:-1], SSM_GROUPS, -1)
    hg = hg * lax.rsqrt(jnp.mean(jnp.square(hg), axis=-1, keepdims=True) + RMS_EPS)
    return (hg.reshape(h.shape) * g).astype(y.dtype)


def ssd_chunked(x, dt, a, b_mat, c_mat):
    bsz, seq = x.shape[0], x.shape[1]
    q = SSM_CHUNK
    nc = seq // q
    g, r, p, n = SSM_GROUPS, HEADS_PER_GROUP, SSM_HEAD_DIM, SSM_STATE
    x = x.astype(jnp.float32)
    dt = dt.astype(jnp.float32)
    xd = (x * dt[..., None]).reshape(bsz, nc, q, g, r, p)
    a_cum = jnp.cumsum((dt * a).reshape(bsz, nc, q, g, r), axis=2)
    bc = b_mat.astype(jnp.float32).reshape(bsz, nc, q, g, n)
    cc = c_mat.astype(jnp.float32).reshape(bsz, nc, q, g, n)

    causal = jnp.tril(jnp.ones((q, q), dtype=bool))
    seg = a_cum[:, :, :, None] - a_cum[:, :, None, :]
    decay_in = jnp.exp(jnp.where(causal[:, :, None, None], seg, -jnp.inf))
    cb = jnp.einsum("bclgn,bcsgn->bclsg", cc, bc)
    y_diag = jnp.einsum("bclsgr,bcsgrp->bclgrp", cb[..., None] * decay_in, xd)

    decay_to_end = jnp.exp(a_cum[:, :, -1:] - a_cum)
    chunk_states = jnp.einsum("bcsgn,bcsgrp->bcgrpn", bc, xd * decay_to_end[..., None])
    chunk_decay = jnp.exp(a_cum[:, :, -1])

    def step(state, inp):
        dec, new = inp
        return state * dec[..., None, None] + new, state

    init = jnp.zeros((bsz, g, r, p, n), jnp.float32)
    _, prev = lax.scan(step, init, (jnp.moveaxis(chunk_decay, 1, 0), jnp.moveaxis(chunk_states, 1, 0)))
    prev = jnp.moveaxis(prev, 0, 1)

    y_off = jnp.einsum("bclgn,bcgrpn->bclgrp", cc, prev) * jnp.exp(a_cum)[..., None]
    return (y_diag + y_off).reshape(bsz, seq, SSM_HEADS * p)


def setup_inputs(seed: int = 0) -> dict:
    key = jax.random.key(seed)
    ks = jax.random.split(key, 20)
    f32 = jnp.float32
    nrm = lambda k, shape, s: jax.random.normal(k, shape, f32) * s
    x = jax.random.normal(ks[0], (BATCH, SEQ, D_MODEL), f32)
    ln_in_g = 1.0 + nrm(ks[1], (D_MODEL,), 0.02)
    ln_in_b = nrm(ks[2], (D_MODEL,), 0.02)
    w_in = nrm(ks[3], (DEPTH, D_MODEL, N_IN), D_MODEL ** -0.5)
    conv_a_w = nrm(ks[4], (DEPTH, CONV_A_WIDTH, D_CONV_BR), CONV_A_WIDTH ** -0.5)
    w_a_out = nrm(ks[5], (DEPTH, D_CONV_BR, D_MODEL), DEEPNORM_BETA * D_CONV_BR ** -0.5)
    conv_s_w = nrm(ks[6], (DEPTH, SSM_CONV_WIDTH, D_XBC), SSM_CONV_WIDTH ** -0.5)
    conv_s_b = nrm(ks[7], (DEPTH, D_XBC), 0.01)
    u = jax.random.uniform(ks[8], (DEPTH, SSM_HEADS), f32)
    dt0 = jnp.exp(u * (math.log(DT_MAX) - math.log(DT_MIN)) + math.log(DT_MIN))
    dt_bias = dt0 + jnp.log(-jnp.expm1(-dt0))
    a_log = jnp.log(jax.random.uniform(ks[9], (DEPTH, SSM_HEADS), f32, A_INIT_MIN, A_INIT_MAX))
    d_skip = 1.0 + nrm(ks[10], (DEPTH, SSM_HEADS), 0.1)
    norm_s_g = 1.0 + nrm(ks[11], (DEPTH, D_SSM), 0.02)
    w_s_out = nrm(ks[12], (DEPTH, D_SSM, D_MODEL), DEEPNORM_BETA * D_SSM ** -0.5)
    w_o = nrm(ks[13], (DEPTH, D_MODEL, D_MODEL), DEEPNORM_BETA * D_MODEL ** -0.5)
    ln_g = 1.0 + nrm(ks[14], (DEPTH, D_MODEL), 0.02)
    ln_b = nrm(ks[15], (DEPTH, D_MODEL), 0.02)
    return {"x": x, "ln_in_g": ln_in_g, "ln_in_b": ln_in_b, "w_in": w_in,
            "conv_a_w": conv_a_w, "w_a_out": w_a_out, "conv_s_w": conv_s_w, "conv_s_b": conv_s_b,
            "dt_bias": dt_bias, "a_log": a_log, "d_skip": d_skip, "norm_s_g": norm_s_g,
            "w_s_out": w_s_out, "w_o": w_o, "ln_g": ln_g, "ln_b": ln_b}


def reference(x, ln_in_g, ln_in_b, w_in, conv_a_w, w_a_out, conv_s_w, conv_s_b,
              dt_bias, a_log, d_skip, norm_s_g, w_s_out, w_o, ln_g, ln_b):
    bsz, seq = x.shape[0], x.shape[1]
    offsets = np.cumsum(SPLIT_SIZES)[:-1].tolist()
    bc_split = [D_SSM, D_SSM + SSM_GROUPS * SSM_STATE]
    h = layer_norm(x, ln_in_g, ln_in_b)
    for i in range(DEPTH):
        proj = h @ w_in[i]
        u, b_gate, c_gate, z_a, z_s, xbc, dt_raw, g_a, g_s = jnp.split(proj, offsets, axis=-1)

        y_a = jax.nn.silu(z_a) * b_gate * causal_depthwise_conv(c_gate * u, conv_a_w[i])
        y_a = y_a @ w_a_out[i]

        xbc = jax.nn.silu(causal_depthwise_conv(xbc, conv_s_w[i]) + conv_s_b[i])
        xs, bs, cs = jnp.split(xbc, bc_split, axis=-1)
        dt = jax.nn.softplus(dt_raw.astype(jnp.float32) + dt_bias[i].astype(jnp.float32))
        a = -jnp.exp(a_log[i].astype(jnp.float32))
        xs_h = xs.reshape(bsz, seq, SSM_HEADS, SSM_HEAD_DIM)
        y_s = ssd_chunked(xs_h, dt, a,
                          bs.reshape(bsz, seq, SSM_GROUPS, SSM_STATE),
                          cs.reshape(bsz, seq, SSM_GROUPS, SSM_STATE))
        y_s = y_s + (xs_h.astype(jnp.float32) * d_skip[i].astype(jnp.float32)[:, None]).reshape(bsz, seq, D_SSM)
        y_s = gated_group_rmsnorm(y_s.astype(h.dtype), z_s, norm_s_g[i]) @ w_s_out[i]

        mixed = jax.nn.sigmoid(g_a) * y_a + jax.nn.sigmoid(g_s) * y_s
        out = mixed @ w_o[i]
        h = layer_norm(DEEPNORM_ALPHA * h + out, ln_g[i], ln_b[i])
    return h
```

```python
import functools

import jax
import jax.numpy as jnp
from jax import lax
from jax.experimental import pallas as pl
from jax.experimental.pallas import tpu as pltpu

D_MODEL = 1024
DEPTH = 2
D_CONV_BR = D_MODEL
CONV_A_WIDTH = 3
D_SSM = 2 * D_MODEL
SSM_HEAD_DIM = 64
SSM_HEADS = D_SSM // SSM_HEAD_DIM
SSM_GROUPS = 4
HEADS_PER_GROUP = SSM_HEADS // SSM_GROUPS
SSM_STATE = 128
SSM_CONV_WIDTH = 4
SSM_CHUNK = 128
D_BC = SSM_GROUPS * SSM_STATE
D_XBC = D_SSM + 2 * D_BC
GROUP_W = HEADS_PER_GROUP * SSM_HEAD_DIM
DEEPNORM_ALPHA = (2 * DEPTH) ** 0.25
LN_EPS = 1e-5
RMS_EPS = 1e-5

OFF_U = 0
OFF_BG = OFF_U + D_CONV_BR
OFF_CG = OFF_BG + D_CONV_BR
OFF_ZA = OFF_CG + D_CONV_BR
OFF_ZS = OFF_ZA + D_CONV_BR
OFF_XBC = OFF_ZS + D_SSM
OFF_GA = OFF_XBC + D_XBC
OFF_GS = OFF_GA + D_MODEL
N_MAIN = OFF_GS + D_MODEL
DT_COL_IN_W = OFF_GA

V7X_LANES = 128
V7X_SUBLANES = 8
V7X_VMEM_LIMIT_BYTES = 56 * 1024 * 1024

DT_W = V7X_LANES
DT_REP = 3
HALO = V7X_SUBLANES

F32 = jnp.float32
BF16 = jnp.bfloat16


def _layer_norm(x, g, b):
    mu = jnp.mean(x, axis=-1, keepdims=True)
    xc = x - mu
    var = jnp.mean(xc * xc, axis=-1, keepdims=True)
    return xc * lax.rsqrt(var + LN_EPS) * g + b


def _silu(x):
    return x * jax.nn.sigmoid(x)


def _in_proj_kernel(*refs, apply_ln):
    if apply_ln:
        x_ref, g_ref, b_ref, w_ref, wdt_ref, proj_ref, dt_ref, h_ref, hb_ref = refs
    else:
        x_ref, w_ref, wdt_ref, proj_ref, dt_ref, hb_ref = refs

    @pl.when(pl.program_id(1) == 0)
    def _():
        x = x_ref[...]
        if apply_ln:
            x = _layer_norm(x, g_ref[...], b_ref[...])
            h_ref[...] = x
        hb = x.astype(BF16)
        hb_ref[...] = hb
        dt_ref[...] = jnp.dot(hb, wdt_ref[...], preferred_element_type=F32)

    proj_ref[...] = jnp.dot(hb_ref[...], w_ref[...],
                            preferred_element_type=F32).astype(proj_ref.dtype)


def _in_proj(x, w_main, w_dt, ln_g=None, ln_b=None, *, tm, tn):
    t = x.shape[0]
    apply_ln = ln_g is not None
    grid = (t // tm, N_MAIN // tn)
    row = lambda i, j: (i, 0)
    const = lambda i, j: (0, 0)
    in_specs = [pl.BlockSpec((tm, D_MODEL), row)]
    args = [x]
    if apply_ln:
        in_specs += [pl.BlockSpec((1, D_MODEL), const), pl.BlockSpec((1, D_MODEL), const)]
        args += [ln_g, ln_b]
    in_specs += [pl.BlockSpec((D_MODEL, tn), lambda i, j: (0, j)),
                 pl.BlockSpec((D_MODEL, DT_W), const)]
    args += [w_main, w_dt]
    out_shape = [jax.ShapeDtypeStruct((t, N_MAIN), BF16), jax.ShapeDtypeStruct((t, DT_W), F32)]
    out_specs = [pl.BlockSpec((tm, tn), lambda i, j: (i, j)), pl.BlockSpec((tm, DT_W), row)]
    if apply_ln:
        out_shape.append(jax.ShapeDtypeStruct((t, D_MODEL), F32))
        out_specs.append(pl.BlockSpec((tm, D_MODEL), row))
    return pl.pallas_call(
        functools.partial(_in_proj_kernel, apply_ln=apply_ln),
        grid=grid,
        in_specs=in_specs,
        out_specs=out_specs,
        out_shape=out_shape,
        scratch_shapes=[pltpu.VMEM((tm, D_MODEL), BF16)],
        compiler_params=pltpu.CompilerParams(
            dimension_semantics=("arbitrary", "arbitrary"),
            vmem_limit_bytes=V7X_VMEM_LIMIT_BYTES),
        name="in_proj_ln" if apply_ln else "in_proj",
    )(*args)


def _split3_by_lane_group(v, lane):
    hi = v.astype(BF16)
    r1 = v - hi.astype(F32)
    mid = r1.astype(BF16)
    lo = (r1 - mid.astype(F32)).astype(BF16)
    return jnp.where(lane < SSM_HEADS, hi, jnp.where(lane < 2 * SSM_HEADS, mid, lo))


def _mixer_kernel(proj_ref, dt_ref, h_ref, caw_ref, csw_ref, csb_ref, dtb_ref, alog_ref,
                  dskip_ref, ng_ref, wa_ref, ws_ref, wo_ref, lng_ref, lnb_ref,
                  expand_ref, tril_ref,
                  out_ref,
                  state_ref, xpad_ref, cupad_ref, conv_ref, ys_ref, *, tq):
    q = SSM_CHUNK

    @pl.when(pl.program_id(1) == 0)
    def _():
        state_ref[...] = jnp.zeros_like(state_ref)
        xpad_ref[0:HALO, :] = jnp.zeros((HALO, D_XBC), F32)
        cupad_ref[0:HALO, :] = jnp.zeros((HALO, D_CONV_BR), F32)

    u = proj_ref[:, OFF_U:OFF_U + D_CONV_BR].astype(F32)
    cg = proj_ref[:, OFF_CG:OFF_CG + D_CONV_BR].astype(F32)
    cupad_ref[HALO:HALO + tq, :] = cg * u
    conv_a = caw_ref[CONV_A_WIDTH - 1:CONV_A_WIDTH, :] * cupad_ref[HALO:HALO + tq, :]
    for k in range(CONV_A_WIDTH - 1):
        sh = CONV_A_WIDTH - 1 - k
        conv_a = conv_a + caw_ref[k:k + 1, :] * cupad_ref[HALO - sh:HALO - sh + tq, :]
    cupad_ref[0:HALO, :] = cupad_ref[tq:tq + HALO, :]
    za = proj_ref[:, OFF_ZA:OFF_ZA + D_CONV_BR].astype(F32)
    bg = proj_ref[:, OFF_BG:OFF_BG + D_CONV_BR].astype(F32)
    ya_pre = (_silu(za) * bg * conv_a).astype(BF16)
    y_a = jnp.dot(ya_pre, wa_ref[...], preferred_element_type=F32)

    cw = 512
    for c0 in range(0, D_XBC, cw):
        cols = slice(c0, c0 + cw)
        xpad_ref[HALO:HALO + tq, cols] = proj_ref[:, OFF_XBC + c0:OFF_XBC + c0 + cw].astype(F32)
        acc = csb_ref[:, cols] + csw_ref[SSM_CONV_WIDTH - 1:SSM_CONV_WIDTH, cols] * xpad_ref[HALO:HALO + tq, cols]
        for k in range(SSM_CONV_WIDTH - 1):
            sh = SSM_CONV_WIDTH - 1 - k
            acc = acc + csw_ref[k:k + 1, cols] * xpad_ref[HALO - sh:HALO - sh + tq, cols]
        conv_ref[:, cols] = _silu(acc)
        xpad_ref[0:HALO, cols] = xpad_ref[tq:tq + HALO, cols]

    lane = lax.broadcasted_iota(jnp.int32, (q, DT_W), 1)
    causal = (lax.broadcasted_iota(jnp.int32, (q, q), 0)
              >= lax.broadcasted_iota(jnp.int32, (q, q), 1))
    lane_pair = lax.broadcasted_iota(jnp.int32, (q, 2 * SSM_HEAD_DIM), 1)
    a_neg = -jnp.exp(alog_ref[...])
    for r0 in range(0, tq, q):
        rows = slice(r0, r0 + q)
        xdt = dt_ref[rows, :] + dtb_ref[...]
        dt = jnp.maximum(xdt, 0.0) + jnp.log1p(jnp.exp(-jnp.abs(xdt)))
        a = dt * a_neg
        a_hi = a.astype(BF16)
        a_r1 = a - a_hi.astype(F32)
        a_mid = a_r1.astype(BF16)
        a_lo = (a_r1 - a_mid.astype(F32)).astype(BF16)
        cs = jnp.dot(tril_ref[...], jnp.concatenate([a_hi, a_mid, a_lo], axis=1),
                     preferred_element_type=F32)
        a_cum = cs[:, 0:DT_W] + cs[:, DT_W:2 * DT_W] + cs[:, 2 * DT_W:3 * DT_W]
        a_end = a_cum[q - 1:q, :]
        ea = jnp.exp(a_cum)
        wst = dt * jnp.exp(a_end - a_cum)
        pieces = jnp.concatenate([_split3_by_lane_group(wst, lane),
                                  _split3_by_lane_group(ea, lane)], axis=0)
        expd = jnp.dot(pieces, expand_ref[...], preferred_element_type=F32)
        wst_x = expd[0:q, :]
        ea_x = expd[q:2 * q, :]
        a_cum_t = a_cum.T
        dt_t = dt.T

        for g in range(SSM_GROUPS):
            gcols = slice(g * GROUP_W, (g + 1) * GROUP_W)
            b_g = conv_ref[rows, D_SSM + g * SSM_STATE:D_SSM + (g + 1) * SSM_STATE].astype(BF16)
            c_g = conv_ref[rows, D_SSM + D_BC + g * SSM_STATE:D_SSM + D_BC + (g + 1) * SSM_STATE].astype(BF16)
            cb = lax.dot_general(c_g, b_g, (((1,), (1,)), ((), ())), preferred_element_type=F32)
            st = state_ref[g]
            x_g = conv_ref[rows, gcols]
            y_off = jnp.dot(c_g, st.astype(BF16), preferred_element_type=F32) * ea_x[:, gcols]
            for j in range(HEADS_PER_GROUP // 2):
                l_pair = []
                for hh in range(2):
                    hd = g * HEADS_PER_GROUP + 2 * j + hh
                    seg = a_cum[:, hd:hd + 1] - a_cum_t[hd:hd + 1, :]
                    dec = jnp.exp(jnp.where(causal, seg, -jnp.inf))
                    l_pair.append((cb * dec * dt_t[hd:hd + 1, :]).astype(BF16))
                pc = slice(j * 2 * SSM_HEAD_DIM, (j + 1) * 2 * SSM_HEAD_DIM)
                x_p = x_g[:, pc]
                x_pb = x_p.astype(BF16)
                zero = jnp.zeros_like(x_pb)
                w_p = jnp.concatenate([jnp.where(lane_pair < SSM_HEAD_DIM, x_pb, zero),
                                       jnp.where(lane_pair >= SSM_HEAD_DIM, x_pb, zero)], axis=0)
                y_diag = jnp.dot(jnp.concatenate(l_pair, axis=1), w_p, preferred_element_type=F32)
                ac = slice(g * GROUP_W + j * 2 * SSM_HEAD_DIM, g * GROUP_W + (j + 1) * 2 * SSM_HEAD_DIM)
                ys_ref[rows, ac] = y_diag + y_off[:, pc] + x_p * dskip_ref[:, ac]
            xw = (x_g * wst_x[:, gcols]).astype(BF16)
            new = lax.dot_general(b_g, xw, (((0,), (0,)), ((), ())), preferred_element_type=F32)
            state_ref[g] = st * ea_x[q - 1:q, gcols] + new

    parts = []
    for g in range(SSM_GROUPS):
        gcols = slice(g * GROUP_W, (g + 1) * GROUP_W)
        zs = proj_ref[:, OFF_ZS + g * GROUP_W:OFF_ZS + (g + 1) * GROUP_W].astype(F32)
        hg = ys_ref[:, gcols] * _silu(zs)
        ms = jnp.mean(hg * hg, axis=-1, keepdims=True)
        parts.append((hg * lax.rsqrt(ms + RMS_EPS) * ng_ref[:, gcols]).astype(BF16))
    y_s = jnp.dot(jnp.concatenate(parts, axis=1), ws_ref[...], preferred_element_type=F32)

    ga = proj_ref[:, OFF_GA:OFF_GA + D_MODEL].astype(F32)
    gs = proj_ref[:, OFF_GS:OFF_GS + D_MODEL].astype(F32)
    mixed = jax.nn.sigmoid(ga) * y_a + jax.nn.sigmoid(gs) * y_s
    out = jnp.dot(mixed.astype(BF16), wo_ref[...], preferred_element_type=F32)
    out_ref[...] = _layer_norm(DEEPNORM_ALPHA * h_ref[...] + out, lng_ref[...], lnb_ref[...])


def _mixer(proj, dt, h, p, *, bsz, seq, tq):
    t = bsz * seq
    nt = seq // tq
    row = lambda b, c: (b * nt + c, 0)
    const2 = lambda b, c: (0, 0)

    def full(a):
        return pl.BlockSpec(a.shape, const2)

    params = [p["conv_a_w"], p["conv_s_w"], p["conv_s_b"], p["dt_bias"], p["a_log"], p["d_skip"],
              p["norm_s_g"], p["w_a_out"], p["w_s_out"], p["w_o"], p["ln_g"], p["ln_b"],
              p["expand"], p["tril"]]
    return pl.pallas_call(
        functools.partial(_mixer_kernel, tq=tq),
        grid=(bsz, nt),
        in_specs=[pl.BlockSpec((tq, N_MAIN), row), pl.BlockSpec((tq, DT_W), row),
                  pl.BlockSpec((tq, D_MODEL), row)] + [full(a) for a in params],
        out_specs=pl.BlockSpec((tq, D_MODEL), row),
        out_shape=jax.ShapeDtypeStruct((t, D_MODEL), F32),
        scratch_shapes=[
            pltpu.VMEM((SSM_GROUPS, SSM_STATE, GROUP_W), F32),
            pltpu.VMEM((HALO + tq, D_XBC), F32),
            pltpu.VMEM((HALO + tq, D_CONV_BR), F32),
            pltpu.VMEM((tq, D_XBC), F32),
            pltpu.VMEM((tq, D_SSM), F32),
        ],
        compiler_params=pltpu.CompilerParams(
            dimension_semantics=("arbitrary", "arbitrary"),
            vmem_limit_bytes=V7X_VMEM_LIMIT_BYTES),
        name="mixer",
    )(proj, dt, h, *params)


def _pad_heads(v):
    return jnp.pad(jnp.tile(v, DT_REP), (0, DT_W - DT_REP * SSM_HEADS)).reshape(1, DT_W)


def kernel(x, ln_in_g, ln_in_b, w_in, conv_a_w, w_a_out, conv_s_w, conv_s_b, dt_bias, a_log, d_skip,
           norm_s_g, w_s_out, w_o, ln_g, ln_b):
    bsz, seq, _ = x.shape
    t = bsz * seq
    tq = 2 * SSM_CHUNK
    tm = min(1024, t)
    tn = N_MAIN // 8

    head_of_row = jnp.arange(DT_W) % SSM_HEADS
    valid_row = jnp.arange(DT_W) < DT_REP * SSM_HEADS
    head_of_col = jnp.arange(D_SSM) // SSM_HEAD_DIM
    expand = ((head_of_row[:, None] == head_of_col[None, :]) & valid_row[:, None]).astype(BF16)
    tril = (jnp.arange(SSM_CHUNK)[:, None] >= jnp.arange(SSM_CHUNK)[None, :]).astype(BF16)

    h = x.reshape(t, D_MODEL)
    for i in range(DEPTH):
        w = w_in[i]
        w_main = jnp.concatenate([w[:, :DT_COL_IN_W], w[:, DT_COL_IN_W + SSM_HEADS:]], axis=1).astype(BF16)
        w_dt = jnp.pad(jnp.tile(w[:, DT_COL_IN_W:DT_COL_IN_W + SSM_HEADS], (1, DT_REP)),
                       ((0, 0), (0, DT_W - DT_REP * SSM_HEADS))).astype(BF16)
        if i == 0:
            proj, dt, h = _in_proj(h, w_main, w_dt, ln_in_g.reshape(1, -1), ln_in_b.reshape(1, -1),
                                   tm=tm, tn=tn)
        else:
            proj, dt = _in_proj(h, w_main, w_dt, tm=tm, tn=tn)
        p = {
            "conv_a_w": conv_a_w[i], "conv_s_w": conv_s_w[i], "conv_s_b": conv_s_b[i].reshape(1, -1),
            "dt_bias": _pad_heads(dt_bias[i]), "a_log": _pad_heads(a_log[i]),
            "d_skip": jnp.repeat(d_skip[i], SSM_HEAD_DIM).reshape(1, -1),
            "norm_s_g": norm_s_g[i].reshape(1, -1),
            "w_a_out": w_a_out[i].astype(BF16), "w_s_out": w_s_out[i].astype(BF16),
            "w_o": w_o[i].astype(BF16),
            "ln_g": ln_g[i].reshape(1, -1), "ln_b": ln_b[i].reshape(1, -1),
            "expand": expand, "tril": tril,
        }
        h = _mixer(proj, dt, h, p, bsz=bsz, seq=seq, tq=tq)
    return h.reshape(bsz, seq, D_MODEL)
```

```python
import functools

import jax
import jax.numpy as jnp
from jax import lax
from jax.experimental import pallas as pl
from jax.experimental.pallas import tpu as pltpu

D_MODEL = 1024
DEPTH = 2
D_CONV_BR = D_MODEL
CONV_A_WIDTH = 3
D_SSM = 2 * D_MODEL
SSM_HEAD_DIM = 64
SSM_HEADS = D_SSM // SSM_HEAD_DIM
SSM_GROUPS = 4
HEADS_PER_GROUP = SSM_HEADS // SSM_GROUPS
SSM_STATE = 128
SSM_CONV_WIDTH = 4
SSM_CHUNK = 128
D_BC = SSM_GROUPS * SSM_STATE
D_XBC = D_SSM + 2 * D_BC
GROUP_W = HEADS_PER_GROUP * SSM_HEAD_DIM
DEEPNORM_ALPHA = (2 * DEPTH) ** 0.25
LN_EPS = 1e-5
RMS_EPS = 1e-5
LOG2E = 1.4426950408889634

OFF_U = 0
OFF_BG = OFF_U + D_CONV_BR
OFF_CG = OFF_BG + D_CONV_BR
OFF_ZA = OFF_CG + D_CONV_BR
OFF_ZS = OFF_ZA + D_CONV_BR
OFF_XBC = OFF_ZS + D_SSM
OFF_GA = OFF_XBC + D_XBC
OFF_GS = OFF_GA + D_MODEL
N_MAIN = OFF_GS + D_MODEL
DT_COL_IN_W = OFF_GA

V7X_LANES = 128
V7X_SUBLANES = 8
V7X_VMEM_LIMIT_BYTES = 56 * 1024 * 1024

DT_W = V7X_LANES
DT_REP = 3
HALO = V7X_SUBLANES

F32 = jnp.float32
BF16 = jnp.bfloat16


def _layer_norm(x, g, b):
    mu = jnp.mean(x, axis=-1, keepdims=True)
    xc = x - mu
    var = jnp.mean(xc * xc, axis=-1, keepdims=True)
    return xc * lax.rsqrt(var + LN_EPS) * g + b


def _silu(x):
    return x * jax.nn.sigmoid(x)


def _in_proj_kernel(*refs, apply_ln):
    if apply_ln:
        x_ref, g_ref, b_ref, w_ref, wdt_ref, proj_ref, dt_ref, h_ref, hb_ref = refs
    else:
        x_ref, w_ref, wdt_ref, proj_ref, dt_ref, hb_ref = refs

    @pl.when(pl.program_id(1) == 0)
    def _():
        x = x_ref[...]
        if apply_ln:
            x = _layer_norm(x, g_ref[...], b_ref[...])
            h_ref[...] = x
        hb = x.astype(BF16)
        hb_ref[...] = hb
        dt_ref[...] = jnp.dot(hb, wdt_ref[...], preferred_element_type=F32)

    proj_ref[...] = jnp.dot(hb_ref[...], w_ref[...],
                            preferred_element_type=F32).astype(proj_ref.dtype)


def _in_proj(x, w_main, w_dt, ln_g=None, ln_b=None, *, tm, tn):
    t = x.shape[0]
    apply_ln = ln_g is not None
    grid = (t // tm, N_MAIN // tn)
    row = lambda i, j: (i, 0)
    const = lambda i, j: (0, 0)
    in_specs = [pl.BlockSpec((tm, D_MODEL), row)]
    args = [x]
    if apply_ln:
        in_specs += [pl.BlockSpec((1, D_MODEL), const), pl.BlockSpec((1, D_MODEL), const)]
        args += [ln_g, ln_b]
    in_specs += [pl.BlockSpec((D_MODEL, tn), lambda i, j: (0, j)),
                 pl.BlockSpec((D_MODEL, DT_W), const)]
    args += [w_main, w_dt]
    out_shape = [jax.ShapeDtypeStruct((t, N_MAIN), BF16), jax.ShapeDtypeStruct((t, DT_W), F32)]
    out_specs = [pl.BlockSpec((tm, tn), lambda i, j: (i, j)), pl.BlockSpec((tm, DT_W), row)]
    if apply_ln:
        out_shape.append(jax.ShapeDtypeStruct((t, D_MODEL), F32))
        out_specs.append(pl.BlockSpec((tm, D_MODEL), row))
    return pl.pallas_call(
        functools.partial(_in_proj_kernel, apply_ln=apply_ln),
        grid=grid,
        in_specs=in_specs,
        out_specs=out_specs,
        out_shape=out_shape,
        scratch_shapes=[pltpu.VMEM((tm, D_MODEL), BF16)],
        compiler_params=pltpu.CompilerParams(
            dimension_semantics=("arbitrary", "arbitrary"),
            vmem_limit_bytes=V7X_VMEM_LIMIT_BYTES),
        name="in_proj_ln" if apply_ln else "in_proj",
    )(*args)


def _split3_by_lane_group(v, lane):
    hi = v.astype(BF16)
    r1 = v - hi.astype(F32)
    mid = r1.astype(BF16)
    lo = (r1 - mid.astype(F32)).astype(BF16)
    return jnp.where(lane < SSM_HEADS, hi, jnp.where(lane < 2 * SSM_HEADS, mid, lo))


def _causal_conv_tile(x, pad_ref, j, w_ref, b_ref, cols, width, tq):
    pad_ref[j, HALO:HALO + tq, :] = x
    acc = w_ref[width - 1:width, cols] * x
    if b_ref is not None:
        acc = acc + b_ref[:, cols]
    for k in range(width - 1):
        sh = width - 1 - k
        acc = acc + w_ref[k:k + 1, cols] * pad_ref[j, HALO - sh:HALO - sh + tq, :]
    pad_ref[j, 0:HALO, :] = pad_ref[j, tq:tq + HALO, :]
    return acc


def _mixer_kernel(proj_ref, dt_ref, h_ref, caw_ref, csw_ref, csb_ref, dtb_ref, alog_ref,
                  dskip_ref, ng_ref, wa_ref, ws_ref, wo_ref, lng_ref, lnb_ref,
                  expand_ref, tril_ref,
                  out_ref,
                  state_ref, xpad_ref, cupad_ref, conv_ref, ys_ref, *, tq):
    q = SSM_CHUNK

    @pl.when(pl.program_id(1) == 0)
    def _():
        state_ref[...] = jnp.zeros_like(state_ref)
        xpad_ref[:, 0:HALO, :] = jnp.zeros((D_XBC // V7X_LANES, HALO, V7X_LANES), F32)
        cupad_ref[:, 0:HALO, :] = jnp.zeros((D_CONV_BR // V7X_LANES, HALO, V7X_LANES), F32)

    ya_parts = []
    for j in range(D_CONV_BR // V7X_LANES):
        cols = slice(j * V7X_LANES, (j + 1) * V7X_LANES)
        u = proj_ref[:, OFF_U + j * V7X_LANES:OFF_U + (j + 1) * V7X_LANES].astype(F32)
        cg = proj_ref[:, OFF_CG + j * V7X_LANES:OFF_CG + (j + 1) * V7X_LANES].astype(F32)
        conv_a = _causal_conv_tile(cg * u, cupad_ref, j, caw_ref, None, cols, CONV_A_WIDTH, tq)
        za = proj_ref[:, OFF_ZA + j * V7X_LANES:OFF_ZA + (j + 1) * V7X_LANES].astype(F32)
        bg = proj_ref[:, OFF_BG + j * V7X_LANES:OFF_BG + (j + 1) * V7X_LANES].astype(F32)
        ya_parts.append((_silu(za) * bg * conv_a).astype(BF16))
    y_a = jnp.dot(jnp.concatenate(ya_parts, axis=1), wa_ref[...], preferred_element_type=F32)

    for j in range(D_XBC // V7X_LANES):
        cols = slice(j * V7X_LANES, (j + 1) * V7X_LANES)
        x_in = proj_ref[:, OFF_XBC + j * V7X_LANES:OFF_XBC + (j + 1) * V7X_LANES].astype(F32)
        conv_ref[:, cols] = _silu(_causal_conv_tile(x_in, xpad_ref, j, csw_ref, csb_ref, cols,
                                                    SSM_CONV_WIDTH, tq))

    lane = lax.broadcasted_iota(jnp.int32, (q, DT_W), 1)
    causal = (lax.broadcasted_iota(jnp.int32, (q, q), 0)
              >= lax.broadcasted_iota(jnp.int32, (q, q), 1))
    lane_pair = lax.broadcasted_iota(jnp.int32, (q, 2 * SSM_HEAD_DIM), 1)
    a_neg = -jnp.exp(alog_ref[...])
    for r0 in range(0, tq, q):
        rows = slice(r0, r0 + q)
        xdt = dt_ref[rows, :] + dtb_ref[...]
        dt = jnp.maximum(xdt, 0.0) + jnp.log1p(jnp.exp(-jnp.abs(xdt)))
        a = dt * a_neg
        a_hi = a.astype(BF16)
        a_r1 = a - a_hi.astype(F32)
        a_mid = a_r1.astype(BF16)
        a_lo = (a_r1 - a_mid.astype(F32)).astype(BF16)
        cs = jnp.dot(tril_ref[...], jnp.concatenate([a_hi, a_mid, a_lo], axis=1),
                     preferred_element_type=F32)
        a_cum = cs[:, 0:DT_W] + cs[:, DT_W:2 * DT_W] + cs[:, 2 * DT_W:3 * DT_W]
        a2 = a_cum * LOG2E
        ea = jnp.exp2(a2)
        wst = dt * jnp.exp2(a2[q - 1:q, :] - a2)
        pieces = jnp.concatenate([_split3_by_lane_group(wst, lane),
                                  _split3_by_lane_group(ea, lane)], axis=0)
        expd = jnp.dot(pieces, expand_ref[...], preferred_element_type=F32)
        wst_x = expd[0:q, :]
        ea_x = expd[q:2 * q, :]
        row2_t = (a2 - jnp.log(dt) * LOG2E).T

        for g in range(SSM_GROUPS):
            gcols = slice(g * GROUP_W, (g + 1) * GROUP_W)
            b_g = conv_ref[rows, D_SSM + g * SSM_STATE:D_SSM + (g + 1) * SSM_STATE].astype(BF16)
            c_g = conv_ref[rows, D_SSM + D_BC + g * SSM_STATE:D_SSM + D_BC + (g + 1) * SSM_STATE].astype(BF16)
            cb = lax.dot_general(c_g, b_g, (((1,), (1,)), ((), ())), preferred_element_type=F32)
            st = state_ref[g]
            x_g = conv_ref[rows, gcols]
            y_off = jnp.dot(c_g, st.astype(BF16), preferred_element_type=F32) * ea_x[:, gcols]
            for j in range(HEADS_PER_GROUP // 2):
                l_pair = []
                for hh in range(2):
                    hd = g * HEADS_PER_GROUP + 2 * j + hh
                    seg = a2[:, hd:hd + 1] - row2_t[hd:hd + 1, :]
                    dec = jnp.exp2(jnp.where(causal, seg, -jnp.inf))
                    l_pair.append((cb * dec).astype(BF16))
                pc = slice(j * 2 * SSM_HEAD_DIM, (j + 1) * 2 * SSM_HEAD_DIM)
                x_p = x_g[:, pc]
                x_pb = x_p.astype(BF16)
                zero = jnp.zeros_like(x_pb)
                w_p = jnp.concatenate([jnp.where(lane_pair < SSM_HEAD_DIM, x_pb, zero),
                                       jnp.where(lane_pair >= SSM_HEAD_DIM, x_pb, zero)], axis=0)
                y_diag = jnp.dot(jnp.concatenate(l_pair, axis=1), w_p, preferred_element_type=F32)
                ac = slice(g * GROUP_W + j * 2 * SSM_HEAD_DIM, g * GROUP_W + (j + 1) * 2 * SSM_HEAD_DIM)
                ys_ref[rows, ac] = y_diag + y_off[:, pc] + x_p * dskip_ref[:, ac]
            xw = (x_g * wst_x[:, gcols]).astype(BF16)
            new = lax.dot_general(b_g, xw, (((0,), (0,)), ((), ())), preferred_element_type=F32)
            state_ref[g] = st * ea_x[q - 1:q, gcols] + new

    parts = []
    for g in range(SSM_GROUPS):
        gcols = slice(g * GROUP_W, (g + 1) * GROUP_W)
        zs = proj_ref[:, OFF_ZS + g * GROUP_W:OFF_ZS + (g + 1) * GROUP_W].astype(F32)
        hg = ys_ref[:, gcols] * _silu(zs)
        ms = jnp.mean(hg * hg, axis=-1, keepdims=True)
        parts.append((hg * lax.rsqrt(ms + RMS_EPS) * ng_ref[:, gcols]).astype(BF16))
    y_s = jnp.dot(jnp.concatenate(parts, axis=1), ws_ref[...], preferred_element_type=F32)

    ga = proj_ref[:, OFF_GA:OFF_GA + D_MODEL].astype(F32)
    gs = proj_ref[:, OFF_GS:OFF_GS + D_MODEL].astype(F32)
    mixed = jax.nn.sigmoid(ga) * y_a + jax.nn.sigmoid(gs) * y_s
    out = jnp.dot(mixed.astype(BF16), wo_ref[...], preferred_element_type=F32)
    out_ref[...] = _layer_norm(DEEPNORM_ALPHA * h_ref[...] + out, lng_ref[...], lnb_ref[...])


def _mixer(proj, dt, h, p, *, bsz, seq, tq):
    t = bsz * seq
    nt = seq // tq
    row = lambda b, c: (b * nt + c, 0)
    const2 = lambda b, c: (0, 0)

    def full(a):
        return pl.BlockSpec(a.shape, const2)

    params = [p["conv_a_w"], p["conv_s_w"], p["conv_s_b"], p["dt_bias"], p["a_log"], p["d_skip"],
              p["norm_s_g"], p["w_a_out"], p["w_s_out"], p["w_o"], p["ln_g"], p["ln_b"],
              p["expand"], p["tril"]]
    return pl.pallas_call(
        functools.partial(_mixer_kernel, tq=tq),
        grid=(bsz, nt),
        in_specs=[pl.BlockSpec((tq, N_MAIN), row), pl.BlockSpec((tq, DT_W), row),
                  pl.BlockSpec((tq, D_MODEL), row)] + [full(a) for a in params],
        out_specs=pl.BlockSpec((tq, D_MODEL), row),
        out_shape=jax.ShapeDtypeStruct((t, D_MODEL), F32),
        scratch_shapes=[
            pltpu.VMEM((SSM_GROUPS, SSM_STATE, GROUP_W), F32),
            pltpu.VMEM((D_XBC // V7X_LANES, HALO + tq, V7X_LANES), F32),
            pltpu.VMEM((D_CONV_BR // V7X_LANES, HALO + tq, V7X_LANES), F32),
            pltpu.VMEM((tq, D_XBC), F32),
            pltpu.VMEM((tq, D_SSM), F32),
        ],
        compiler_params=pltpu.CompilerParams(
            dimension_semantics=("arbitrary", "arbitrary"),
            vmem_limit_bytes=V7X_VMEM_LIMIT_BYTES),
        name="mixer",
    )(proj, dt, h, *params)


def _pad_heads(v):
    return jnp.pad(jnp.tile(v, DT_REP), (0, DT_W - DT_REP * SSM_HEADS)).reshape(1, DT_W)


def kernel(x, ln_in_g, ln_in_b, w_in, conv_a_w, w_a_out, conv_s_w, conv_s_b, dt_bias, a_log, d_skip,
           norm_s_g, w_s_out, w_o, ln_g, ln_b):
    bsz, seq, _ = x.shape
    t = bsz * seq
    tq = 2 * SSM_CHUNK
    tm = min(1024, t)
    tn = N_MAIN // 4

    head_of_row = jnp.arange(DT_W) % SSM_HEADS
    valid_row = jnp.arange(DT_W) < DT_REP * SSM_HEADS
    head_of_col = jnp.arange(D_SSM) // SSM_HEAD_DIM
    expand = ((head_of_row[:, None] == head_of_col[None, :]) & valid_row[:, None]).astype(BF16)
    tril = (jnp.arange(SSM_CHUNK)[:, None] >= jnp.arange(SSM_CHUNK)[None, :]).astype(BF16)

    h = x.reshape(t, D_MODEL)
    for i in range(DEPTH):
        w = w_in[i]
        w_main = jnp.concatenate([w[:, :DT_COL_IN_W], w[:, DT_COL_IN_W + SSM_HEADS:]], axis=1).astype(BF16)
        w_dt = jnp.pad(jnp.tile(w[:, DT_COL_IN_W:DT_COL_IN_W + SSM_HEADS], (1, DT_REP)),
                       ((0, 0), (0, DT_W - DT_REP * SSM_HEADS))).astype(BF16)
        if i == 0:
            proj, dt, h = _in_proj(h, w_main, w_dt, ln_in_g.reshape(1, -1), ln_in_b.reshape(1, -1),
                                   tm=tm, tn=tn)
        else:
            proj, dt = _in_proj(h, w_main, w_dt, tm=tm, tn=tn)
        p = {
            "conv_a_w": conv_a_w[i], "conv_s_w": conv_s_w[i], "conv_s_b": conv_s_b[i].reshape(1, -1),
            "dt_bias": _pad_heads(dt_bias[i]), "a_log": _pad_heads(a_log[i]),
            "d_skip": jnp.repeat(d_skip[i], SSM_HEAD_DIM).reshape(1, -1),
            "norm_s_g": norm_s_g[i].reshape(1, -1),
            "w_a_out": w_a_out[i].astype(BF16), "w_s_out": w_s_out[i].astype(BF16),
            "w_o": w_o[i].astype(BF16),
            "ln_g": ln_g[i].reshape(1, -1), "ln_b": ln_b[i].reshape(1, -1),
            "expand": expand, "tril": tril,
        }
        h = _mixer(proj, dt, h, p, bsz=bsz, seq=seq, tq=tq)
    return h.reshape(bsz, seq, D_MODEL)
```

```python
import functools

import jax
import jax.numpy as jnp
from jax import lax
from jax.experimental import pallas as pl
from jax.experimental.pallas import tpu as pltpu

D_MODEL = 1024
DEPTH = 2
D_CONV_BR = D_MODEL
CONV_A_WIDTH = 3
D_SSM = 2 * D_MODEL
SSM_HEAD_DIM = 64
SSM_HEADS = D_SSM // SSM_HEAD_DIM
SSM_GROUPS = 4
HEADS_PER_GROUP = SSM_HEADS // SSM_GROUPS
SSM_STATE = 128
SSM_CONV_WIDTH = 4
SSM_CHUNK = 128
D_BC = SSM_GROUPS * SSM_STATE
D_XBC = D_SSM + 2 * D_BC
GROUP_W = HEADS_PER_GROUP * SSM_HEAD_DIM
DEEPNORM_ALPHA = (2 * DEPTH) ** 0.25
LN_EPS = 1e-5
RMS_EPS = 1e-5
LOG2E = 1.4426950408889634

OFF_U = 0
OFF_BG = OFF_U + D_CONV_BR
OFF_CG = OFF_BG + D_CONV_BR
OFF_ZA = OFF_CG + D_CONV_BR
OFF_ZS = OFF_ZA + D_CONV_BR
OFF_XBC = OFF_ZS + D_SSM
OFF_GA = OFF_XBC + D_XBC
OFF_GS = OFF_GA + D_MODEL
N_MAIN = OFF_GS + D_MODEL
DT_COL_IN_W = OFF_GA

ACT_YA = 0
ACT_ZS = ACT_YA + D_CONV_BR
ACT_GA = ACT_ZS + D_SSM
ACT_GS = ACT_GA + D_MODEL
N_ACT = ACT_GS + D_MODEL

V7X_LANES = 128
V7X_SUBLANES = 8
V7X_VMEM_LIMIT_BYTES = 56 * 1024 * 1024

WIDTH_PAD = V7X_LANES
DT_W = V7X_LANES
DT_REP = 3
NBLK = 512
KBLK = 256
HALO = V7X_SUBLANES

F32 = jnp.float32
BF16 = jnp.bfloat16


def _layer_norm(x, g, b):
    mu = jnp.mean(x, axis=-1, keepdims=True)
    xc = x - mu
    var = jnp.mean(xc * xc, axis=-1, keepdims=True)
    return xc * lax.rsqrt(var + LN_EPS) * g + b


def _silu(x):
    return x * jax.nn.sigmoid(x)


def _conv_block(v, slab_ref, hist_ref, hist0, w_ref, b_ref, col0, width, tm):
    n_t = NBLK // V7X_LANES
    for t in range(n_t):
        slab_ref[t, 0:HALO, :] = hist_ref[hist0 + t]
        slab_ref[t, HALO:HALO + tm, :] = v[:, t * V7X_LANES:(t + 1) * V7X_LANES]
    outs = []
    for t in range(n_t):
        cols = slice(col0 + t * V7X_LANES, col0 + (t + 1) * V7X_LANES)
        acc = None
        for k in range(width):
            sh = width - 1 - k
            term = w_ref[k:k + 1, cols] * slab_ref[t, HALO - sh:HALO - sh + tm, :]
            acc = term if acc is None else acc + term
        if b_ref is not None:
            acc = acc + b_ref[:, cols]
        hist_ref[hist0 + t] = slab_ref[t, tm:tm + HALO, :]
        outs.append(acc)
    return jnp.concatenate(outs, axis=1)


def _in_proj_kernel(*refs, apply_ln, tm, tiles_per_seq):
    if apply_ln:
        (x_ref, g_ref, b_ref, w_ref, caw_ref, csw_ref, csb_ref,
         act_ref, xbc_ref, dt_ref, h_ref, hb_ref, slab_ref, hist_ref, tmp_ref) = refs
    else:
        (x_ref, w_ref, caw_ref, csw_ref, csb_ref,
         act_ref, xbc_ref, dt_ref, hb_ref, slab_ref, hist_ref, tmp_ref) = refs

    @pl.when(pl.program_id(0) % tiles_per_seq == 0)
    def _():
        hist_ref[...] = jnp.zeros_like(hist_ref)

    x = x_ref[...]
    if apply_ln:
        x = _layer_norm(x, g_ref[...], b_ref[...])
        h_ref[...] = x
    hb_ref[...] = x.astype(BF16)
    dt_ref[...] = jnp.dot(hb_ref[...], w_ref[:, N_MAIN:N_MAIN + DT_W], preferred_element_type=F32)
    act_ref[:, N_ACT:N_ACT + WIDTH_PAD] = jnp.zeros((tm, WIDTH_PAD), BF16)
    xbc_ref[:, D_XBC:D_XBC + WIDTH_PAD] = jnp.zeros((tm, WIDTH_PAD), BF16)

    def mm(c0):
        return jnp.dot(hb_ref[...], w_ref[:, c0:c0 + NBLK], preferred_element_type=F32)

    hist_x0 = D_CONV_BR // V7X_LANES
    for c0 in range(0, D_CONV_BR, NBLK):
        tmp_ref[...] = mm(OFF_U + c0)
        conv_a = _conv_block(mm(OFF_CG + c0) * tmp_ref[...], slab_ref, hist_ref, c0 // V7X_LANES,
                             caw_ref, None, c0, CONV_A_WIDTH, tm)
        tmp_ref[...] = _silu(mm(OFF_ZA + c0)) * conv_a
        act_ref[:, ACT_YA + c0:ACT_YA + c0 + NBLK] = (mm(OFF_BG + c0) * tmp_ref[...]).astype(BF16)
    for c0 in range(0, D_SSM, NBLK):
        act_ref[:, ACT_ZS + c0:ACT_ZS + c0 + NBLK] = _silu(mm(OFF_ZS + c0)).astype(BF16)
    for c0 in range(0, D_XBC, NBLK):
        conv_s = _conv_block(mm(OFF_XBC + c0), slab_ref, hist_ref, hist_x0 + c0 // V7X_LANES,
                             csw_ref, csb_ref, c0, SSM_CONV_WIDTH, tm)
        xbc_ref[:, c0:c0 + NBLK] = _silu(conv_s).astype(BF16)
    for c0 in range(0, 2 * D_MODEL, NBLK):
        act_ref[:, ACT_GA + c0:ACT_GA + c0 + NBLK] = jax.nn.sigmoid(mm(OFF_GA + c0)).astype(BF16)


def _in_proj(x, w_main, conv_a_w, conv_s_w, conv_s_b, ln_g=None, ln_b=None, *, tm, seq):
    t = x.shape[0]
    apply_ln = ln_g is not None
    row = lambda i: (i, 0)
    const = lambda i: (0, 0)

    def resident(a):
        return pl.BlockSpec(a.shape, const, pipeline_mode=pl.Buffered(1))

    in_specs = [pl.BlockSpec((tm, D_MODEL), row)]
    args = [x]
    if apply_ln:
        in_specs += [resident(ln_g), resident(ln_b)]
        args += [ln_g, ln_b]
    params = [w_main, conv_a_w, conv_s_w, conv_s_b]
    in_specs += [resident(a) for a in params]
    args += params
    widths = [(N_ACT + WIDTH_PAD, BF16), (D_XBC + WIDTH_PAD, BF16), (DT_W, F32)]
    if apply_ln:
        widths.append((D_MODEL, F32))
    n_hist = (D_CONV_BR + D_XBC) // V7X_LANES
    return pl.pallas_call(
        functools.partial(_in_proj_kernel, apply_ln=apply_ln, tm=tm, tiles_per_seq=seq // tm),
        grid=(t // tm,),
        in_specs=in_specs,
        out_specs=[pl.BlockSpec((tm, w), row) for w, _ in widths],
        out_shape=[jax.ShapeDtypeStruct((t, w), d) for w, d in widths],
        scratch_shapes=[
            pltpu.VMEM((tm, D_MODEL), BF16),
            pltpu.VMEM((NBLK // V7X_LANES, HALO + tm, V7X_LANES), F32),
            pltpu.VMEM((n_hist, HALO, V7X_LANES), F32),
            pltpu.VMEM((tm, NBLK), F32),
        ],
        compiler_params=pltpu.CompilerParams(
            dimension_semantics=("arbitrary",),
            vmem_limit_bytes=V7X_VMEM_LIMIT_BYTES),
        name="in_proj_ln" if apply_ln else "in_proj",
    )(*args)


def _split3_by_lane_group(v, lane):
    hi = v.astype(BF16)
    r1 = v - hi.astype(F32)
    mid = r1.astype(BF16)
    lo = (r1 - mid.astype(F32)).astype(BF16)
    return jnp.where(lane < SSM_HEADS, hi, jnp.where(lane < 2 * SSM_HEADS, mid, lo))


def _ssd_steps(xbc_ref, dt_ref, dtb_ref, alog_ref, dskip_ref, expand_ref, tril_ref,
               state_ref, ys_ref, slot, tq):
    q = SSM_CHUNK
    lane = lax.broadcasted_iota(jnp.int32, (q, DT_W), 1)
    causal = (lax.broadcasted_iota(jnp.int32, (q, q), 0)
              >= lax.broadcasted_iota(jnp.int32, (q, q), 1))
    lane_pair = lax.broadcasted_iota(jnp.int32, (q, 2 * SSM_HEAD_DIM), 1)
    steps = []
    for r0 in range(0, tq, q):
        rows = slice(r0, r0 + q)
        ck = {}

        def prologue(rows=rows, ck=ck):
            a_neg = -jnp.exp(alog_ref[...])
            xdt = dt_ref[rows, :] + dtb_ref[...]
            dt = jnp.maximum(xdt, 0.0) + jnp.log1p(jnp.exp(-jnp.abs(xdt)))
            a = dt * a_neg
            a_hi = a.astype(BF16)
            a_r1 = a - a_hi.astype(F32)
            a_mid = a_r1.astype(BF16)
            a_lo = (a_r1 - a_mid.astype(F32)).astype(BF16)
            cs = jnp.dot(tril_ref[...], jnp.concatenate([a_hi, a_mid, a_lo], axis=1),
                         preferred_element_type=F32)
            a_cum = cs[:, 0:DT_W] + cs[:, DT_W:2 * DT_W] + cs[:, 2 * DT_W:3 * DT_W]
            a2 = a_cum * LOG2E
            ea = jnp.exp2(a2)
            wst = dt * jnp.exp2(a2[q - 1:q, :] - a2)
            pieces = jnp.concatenate([_split3_by_lane_group(wst, lane),
                                      _split3_by_lane_group(ea, lane)], axis=0)
            expd = jnp.dot(pieces, expand_ref[:, 0:D_SSM], preferred_element_type=F32)
            ck["a2"] = a2
            ck["wst_x"] = expd[0:q, :]
            ck["ea_x"] = expd[q:2 * q, :]
            ck["row2_t"] = (a2 - jnp.log(dt) * LOG2E).T

        steps.append(prologue)
        for g in range(SSM_GROUPS):
            def group(g=g, rows=rows, ck=ck):
                a2, wst_x, ea_x, row2_t = ck["a2"], ck["wst_x"], ck["ea_x"], ck["row2_t"]
                gcols = slice(g * GROUP_W, (g + 1) * GROUP_W)
                b_g = xbc_ref[rows, D_SSM + g * SSM_STATE:D_SSM + (g + 1) * SSM_STATE]
                c_g = xbc_ref[rows, D_SSM + D_BC + g * SSM_STATE:D_SSM + D_BC + (g + 1) * SSM_STATE]
                cb = lax.dot_general(c_g, b_g, (((1,), (1,)), ((), ())), preferred_element_type=F32)
                st = state_ref[g]
                x_gb = xbc_ref[rows, gcols]
                x_g = x_gb.astype(F32)
                y_off = jnp.dot(c_g, st.astype(BF16), preferred_element_type=F32) * ea_x[:, gcols]
                for j in range(HEADS_PER_GROUP // 2):
                    l_pair = []
                    for hh in range(2):
                        hd = g * HEADS_PER_GROUP + 2 * j + hh
                        seg = a2[:, hd:hd + 1] - row2_t[hd:hd + 1, :]
                        dec = jnp.exp2(jnp.where(causal, seg, -jnp.inf))
                        l_pair.append((cb * dec).astype(BF16))
                    pc = slice(j * 2 * SSM_HEAD_DIM, (j + 1) * 2 * SSM_HEAD_DIM)
                    x_pb = x_gb[:, pc]
                    zero = jnp.zeros_like(x_pb)
                    w_p = jnp.concatenate([jnp.where(lane_pair < SSM_HEAD_DIM, x_pb, zero),
                                           jnp.where(lane_pair >= SSM_HEAD_DIM, x_pb, zero)], axis=0)
                    y_diag = jnp.dot(jnp.concatenate(l_pair, axis=1), w_p, preferred_element_type=F32)
                    ac = slice(g * GROUP_W + j * 2 * SSM_HEAD_DIM,
                               g * GROUP_W + (j + 1) * 2 * SSM_HEAD_DIM)
                    ys_ref[slot, rows, ac] = y_diag + y_off[:, pc] + x_g[:, pc] * dskip_ref[:, ac]
                xw = (x_g * wst_x[:, gcols]).astype(BF16)
                new = lax.dot_general(b_g, xw, (((0,), (0,)), ((), ())), preferred_element_type=F32)
                state_ref[g] = st * ea_x[q - 1:q, gcols] + new

            steps.append(group)
    return steps


def _proj_steps(act_ref, h_ref, ng_ref, wa_ref, ws_ref, wo_ref, lng_ref, lnb_ref, out_ref,
                ys_ref, slot, ya_ref, ysp_ref, hn_ref, mixed_ref):
    n_blk = D_MODEL // KBLK
    steps = []
    for nb in range(n_blk):
        def branch_a(nb=nb):
            cols = slice(nb * KBLK, (nb + 1) * KBLK)
            ya_ref[:, cols] = jnp.dot(act_ref[:, ACT_YA:ACT_YA + D_CONV_BR], wa_ref[:, cols],
                                      preferred_element_type=F32)
        steps.append(branch_a)
    for g in range(SSM_GROUPS):
        def norm_s(g=g):
            gcols = slice(g * GROUP_W, (g + 1) * GROUP_W)
            hg = ys_ref[slot, :, gcols] * act_ref[:, ACT_ZS + g * GROUP_W:ACT_ZS + (g + 1) * GROUP_W].astype(F32)
            ms = jnp.mean(hg * hg, axis=-1, keepdims=True)
            hn_ref[:, gcols] = (hg * lax.rsqrt(ms + RMS_EPS) * ng_ref[:, gcols]).astype(BF16)
        steps.append(norm_s)
    for nb in range(n_blk):
        def branch_s(nb=nb):
            cols = slice(nb * KBLK, (nb + 1) * KBLK)
            ysp_ref[:, cols] = jnp.dot(hn_ref[:, 0:D_SSM], ws_ref[:, cols], preferred_element_type=F32)
        steps.append(branch_s)
    for nb in range(n_blk):
        def merge(nb=nb):
            cols = slice(nb * KBLK, (nb + 1) * KBLK)
            sig_a = act_ref[:, ACT_GA + nb * KBLK:ACT_GA + (nb + 1) * KBLK].astype(F32)
            sig_s = act_ref[:, ACT_GS + nb * KBLK:ACT_GS + (nb + 1) * KBLK].astype(F32)
            mixed_ref[:, cols] = (sig_a * ya_ref[:, cols] + sig_s * ysp_ref[:, cols]).astype(BF16)
        steps.append(merge)
    for nb in range(n_blk):
        def project(nb=nb):
            cols = slice(nb * KBLK, (nb + 1) * KBLK)
            out_ref[:, cols] = jnp.dot(mixed_ref[:, 0:D_MODEL], wo_ref[:, cols], preferred_element_type=F32)
        steps.append(project)

    def deepnorm():
        out_ref[...] = _layer_norm(DEEPNORM_ALPHA * h_ref[...] + out_ref[...], lng_ref[...], lnb_ref[...])

    steps.append(deepnorm)
    return steps


def _interleave(a, b):
    if len(a) < len(b):
        a, b = b, a
    done = 0
    for i, fa in enumerate(a):
        fa()
        due = ((i + 1) * len(b)) // len(a)
        for fb in b[done:due]:
            fb()
        done = due


def _mixer_kernel(xbc_ref, dt_ref, act_ref, h_ref, dtb_ref, alog_ref,
                  dskip_ref, ng_ref, wa_ref, ws_ref, wo_ref, lng_ref, lnb_ref,
                  expand_ref, tril_ref,
                  out_ref,
                  state_ref, ys_ref, ya_ref, ysp_ref, hn_ref, mixed_ref, *, tq, n_tiles):
    c = pl.program_id(1)
    slot = c % 2

    def ssd():
        return _ssd_steps(xbc_ref, dt_ref, dtb_ref, alog_ref, dskip_ref, expand_ref, tril_ref,
                          state_ref, ys_ref, slot, tq)

    def proj():
        return _proj_steps(act_ref, h_ref, ng_ref, wa_ref, ws_ref, wo_ref, lng_ref, lnb_ref, out_ref,
                           ys_ref, 1 - slot, ya_ref, ysp_ref, hn_ref, mixed_ref)

    @pl.when(c == 0)
    def _():
        state_ref[...] = jnp.zeros_like(state_ref)
        for f in ssd():
            f()

    @pl.when(jnp.logical_and(c > 0, c < n_tiles))
    def _():
        _interleave(ssd(), proj())

    @pl.when(c == n_tiles)
    def _():
        for f in proj():
            f()


def _mixer(xbc, dt, act, h, p, *, bsz, seq, tq):
    t = bsz * seq
    nt = seq // tq
    cur = lambda b, c: (b * nt + jnp.minimum(c, nt - 1), 0)
    prev = lambda b, c: (b * nt + jnp.maximum(c - 1, 0), 0)
    const2 = lambda b, c: (0, 0)

    def full(a):
        return pl.BlockSpec(a.shape, const2, pipeline_mode=pl.Buffered(1))

    params = [p["dt_bias"], p["a_log"], p["d_skip"],
              p["norm_s_g"], p["w_a_out"], p["w_s_out"], p["w_o"], p["ln_g"], p["ln_b"],
              p["expand"], p["tril"]]
    return pl.pallas_call(
        functools.partial(_mixer_kernel, tq=tq, n_tiles=nt),
        grid=(bsz, nt + 1),
        in_specs=[pl.BlockSpec((tq, D_XBC + WIDTH_PAD), cur), pl.BlockSpec((tq, DT_W), cur),
                  pl.BlockSpec((tq, N_ACT + WIDTH_PAD), prev), pl.BlockSpec((tq, D_MODEL), prev)]
                 + [full(a) for a in params],
        out_specs=pl.BlockSpec((tq, D_MODEL), prev),
        out_shape=jax.ShapeDtypeStruct((t, D_MODEL), F32),
        scratch_shapes=[
            pltpu.VMEM((SSM_GROUPS, SSM_STATE, GROUP_W), F32),
            pltpu.VMEM((2, tq, D_SSM), F32),
            pltpu.VMEM((tq, D_MODEL), F32),
            pltpu.VMEM((tq, D_MODEL), F32),
            pltpu.VMEM((tq, D_SSM + WIDTH_PAD), BF16),
            pltpu.VMEM((tq, D_MODEL + WIDTH_PAD), BF16),
        ],
        compiler_params=pltpu.CompilerParams(
            dimension_semantics=("arbitrary", "arbitrary"),
            vmem_limit_bytes=V7X_VMEM_LIMIT_BYTES),
        name="mixer",
    )(xbc, dt, act, h, *params)


def _pad_heads(v):
    return jnp.pad(jnp.tile(v, DT_REP), (0, DT_W - DT_REP * SSM_HEADS)).reshape(1, DT_W)


def _pad_cols(w):
    return jnp.pad(w, ((0, 0), (0, WIDTH_PAD)))


def kernel(x, ln_in_g, ln_in_b, w_in, conv_a_w, w_a_out, conv_s_w, conv_s_b, dt_bias, a_log, d_skip,
           norm_s_g, w_s_out, w_o, ln_g, ln_b):
    bsz, seq, _ = x.shape
    t = bsz * seq
    tq = 2 * SSM_CHUNK
    tm = min(512, seq)

    head_of_row = jnp.arange(DT_W) % SSM_HEADS
    valid_row = jnp.arange(DT_W) < DT_REP * SSM_HEADS
    head_of_col = jnp.arange(D_SSM) // SSM_HEAD_DIM
    expand = _pad_cols(((head_of_row[:, None] == head_of_col[None, :]) & valid_row[:, None]).astype(BF16))
    tril = (jnp.arange(SSM_CHUNK)[:, None] >= jnp.arange(SSM_CHUNK)[None, :]).astype(BF16)

    h = x.reshape(t, D_MODEL)
    for i in range(DEPTH):
        w = w_in[i]
        w_dt = jnp.pad(jnp.tile(w[:, DT_COL_IN_W:DT_COL_IN_W + SSM_HEADS], (1, DT_REP)),
                       ((0, 0), (0, DT_W - DT_REP * SSM_HEADS)))
        w_main = jnp.concatenate([w[:, :DT_COL_IN_W], w[:, DT_COL_IN_W + SSM_HEADS:], w_dt],
                                 axis=1).astype(BF16)
        conv_params = (conv_a_w[i], conv_s_w[i], conv_s_b[i].reshape(1, -1))
        if i == 0:
            act, xbc, dt, h = _in_proj(h, w_main, *conv_params, ln_in_g.reshape(1, -1),
                                       ln_in_b.reshape(1, -1), tm=tm, seq=seq)
        else:
            act, xbc, dt = _in_proj(h, w_main, *conv_params, tm=tm, seq=seq)
        p = {
            "dt_bias": _pad_heads(dt_bias[i]), "a_log": _pad_heads(a_log[i]),
            "d_skip": jnp.repeat(d_skip[i], SSM_HEAD_DIM).reshape(1, -1),
            "norm_s_g": norm_s_g[i].reshape(1, -1),
            "w_a_out": _pad_cols(w_a_out[i].astype(BF16)), "w_s_out": _pad_cols(w_s_out[i].astype(BF16)),
            "w_o": _pad_cols(w_o[i].astype(BF16)),
            "ln_g": ln_g[i].reshape(1, -1), "ln_b": ln_b[i].reshape(1, -1),
            "expand": expand, "tril": tril,
        }
        h = _mixer(xbc, dt, act, h, p, bsz=bsz, seq=seq, tq=tq)
    return h.reshape(bsz, seq, D_MODEL)
```

```python
import functools

import jax
import jax.numpy as jnp
from jax import lax
from jax.experimental import pallas as pl
from jax.experimental.pallas import tpu as pltpu

D_MODEL = 1024
DEPTH = 2
D_CONV_BR = D_MODEL
CONV_A_WIDTH = 3
D_SSM = 2 * D_MODEL
SSM_HEAD_DIM = 64
SSM_HEADS = D_SSM // SSM_HEAD_DIM
SSM_GROUPS = 4
HEADS_PER_GROUP = SSM_HEADS // SSM_GROUPS
SSM_STATE = 128
SSM_CONV_WIDTH = 4
SSM_CHUNK = 128
D_BC = SSM_GROUPS * SSM_STATE
D_XBC = D_SSM + 2 * D_BC
GROUP_W = HEADS_PER_GROUP * SSM_HEAD_DIM
DEEPNORM_ALPHA = (2 * DEPTH) ** 0.25
LN_EPS = 1e-5
RMS_EPS = 1e-5
LOG2E = 1.4426950408889634

OFF_U = 0
OFF_BG = OFF_U + D_CONV_BR
OFF_CG = OFF_BG + D_CONV_BR
OFF_ZA = OFF_CG + D_CONV_BR
OFF_ZS = OFF_ZA + D_CONV_BR
OFF_XBC = OFF_ZS + D_SSM
OFF_GA = OFF_XBC + D_XBC
OFF_GS = OFF_GA + D_MODEL
N_MAIN = OFF_GS + D_MODEL
DT_COL_IN_W = OFF_GA

ACT_YA = 0
ACT_ZS = ACT_YA + D_CONV_BR
ACT_GA = ACT_ZS + D_SSM
ACT_GS = ACT_GA + D_MODEL
N_ACT = ACT_GS + D_MODEL

V7X_LANES = 128
V7X_SUBLANES = 8
V7X_VMEM_LIMIT_BYTES = 56 * 1024 * 1024

WIDTH_PAD = V7X_LANES
DT_W = V7X_LANES
DT_REP = 3
NBLK = 512
KBLK = 256
HALO = V7X_SUBLANES

F32 = jnp.float32
BF16 = jnp.bfloat16


def _layer_norm(x, g, b):
    mu = jnp.mean(x, axis=-1, keepdims=True)
    xc = x - mu
    var = jnp.mean(xc * xc, axis=-1, keepdims=True)
    return xc * lax.rsqrt(var + LN_EPS) * g + b


def _sigmoid(x):
    return 0.5 * jnp.tanh(0.5 * x) + 0.5


def _silu(x):
    hx = 0.5 * x
    return hx * (1.0 + jnp.tanh(hx))


def _conv_block(v, slab_ref, hist_ref, hist0, w_ref, b_ref, col0, width, tm):
    n_t = NBLK // V7X_LANES
    for t in range(n_t):
        slab_ref[t, 0:HALO, :] = hist_ref[hist0 + t]
        slab_ref[t, HALO:HALO + tm, :] = v[:, t * V7X_LANES:(t + 1) * V7X_LANES]
    outs = []
    for t in range(n_t):
        cols = slice(col0 + t * V7X_LANES, col0 + (t + 1) * V7X_LANES)
        acc = None
        for k in range(width):
            sh = width - 1 - k
            term = w_ref[k:k + 1, cols] * slab_ref[t, HALO - sh:HALO - sh + tm, :]
            acc = term if acc is None else acc + term
        if b_ref is not None:
            acc = acc + b_ref[:, cols]
        hist_ref[hist0 + t] = slab_ref[t, tm:tm + HALO, :]
        outs.append(acc)
    return jnp.concatenate(outs, axis=1)


def _in_proj_kernel(*refs, apply_ln, tm, tiles_per_seq):
    if apply_ln:
        (x_ref, g_ref, b_ref, w_ref, caw_ref, csw_ref, csb_ref,
         act_ref, xbc_ref, dt_ref, h_ref, hb_ref, slab_ref, hist_ref, tmp_ref) = refs
    else:
        (x_ref, w_ref, caw_ref, csw_ref, csb_ref,
         act_ref, xbc_ref, dt_ref, hb_ref, slab_ref, hist_ref, tmp_ref) = refs

    @pl.when(pl.program_id(0) % tiles_per_seq == 0)
    def _():
        hist_ref[...] = jnp.zeros_like(hist_ref)

    x = x_ref[...]
    if apply_ln:
        x = _layer_norm(x, g_ref[...], b_ref[...])
        h_ref[...] = x
    hb_ref[...] = x.astype(BF16)
    dt_ref[...] = jnp.dot(hb_ref[...], w_ref[:, N_MAIN:N_MAIN + DT_W], preferred_element_type=F32)
    act_ref[:, N_ACT:N_ACT + WIDTH_PAD] = jnp.zeros((tm, WIDTH_PAD), BF16)
    xbc_ref[:, D_XBC:D_XBC + WIDTH_PAD] = jnp.zeros((tm, WIDTH_PAD), BF16)

    def mm(c0):
        return jnp.dot(hb_ref[...], w_ref[:, c0:c0 + NBLK], preferred_element_type=F32)

    hist_x0 = D_CONV_BR // V7X_LANES
    for c0 in range(0, D_CONV_BR, NBLK):
        tmp_ref[...] = mm(OFF_U + c0)
        conv_a = _conv_block(mm(OFF_CG + c0) * tmp_ref[...], slab_ref, hist_ref, c0 // V7X_LANES,
                             caw_ref, None, c0, CONV_A_WIDTH, tm)
        tmp_ref[...] = _silu(mm(OFF_ZA + c0)) * conv_a
        act_ref[:, ACT_YA + c0:ACT_YA + c0 + NBLK] = (mm(OFF_BG + c0) * tmp_ref[...]).astype(BF16)
    for c0 in range(0, D_SSM, NBLK):
        act_ref[:, ACT_ZS + c0:ACT_ZS + c0 + NBLK] = _silu(mm(OFF_ZS + c0)).astype(BF16)
    for c0 in range(0, D_XBC, NBLK):
        conv_s = _conv_block(mm(OFF_XBC + c0), slab_ref, hist_ref, hist_x0 + c0 // V7X_LANES,
                             csw_ref, csb_ref, c0, SSM_CONV_WIDTH, tm)
        xbc_ref[:, c0:c0 + NBLK] = _silu(conv_s).astype(BF16)
    for c0 in range(0, 2 * D_MODEL, NBLK):
        act_ref[:, ACT_GA + c0:ACT_GA + c0 + NBLK] = _sigmoid(mm(OFF_GA + c0)).astype(BF16)


def _in_proj(x, w_main, conv_a_w, conv_s_w, conv_s_b, ln_g=None, ln_b=None, *, tm, seq):
    t = x.shape[0]
    apply_ln = ln_g is not None
    row = lambda i: (i, 0)
    const = lambda i: (0, 0)

    def resident(a):
        return pl.BlockSpec(a.shape, const, pipeline_mode=pl.Buffered(1))

    in_specs = [pl.BlockSpec((tm, D_MODEL), row)]
    args = [x]
    if apply_ln:
        in_specs += [resident(ln_g), resident(ln_b)]
        args += [ln_g, ln_b]
    params = [w_main, conv_a_w, conv_s_w, conv_s_b]
    in_specs += [resident(a) for a in params]
    args += params
    widths = [(N_ACT + WIDTH_PAD, BF16), (D_XBC + WIDTH_PAD, BF16), (DT_W, F32)]
    if apply_ln:
        widths.append((D_MODEL, F32))
    n_hist = (D_CONV_BR + D_XBC) // V7X_LANES
    return pl.pallas_call(
        functools.partial(_in_proj_kernel, apply_ln=apply_ln, tm=tm, tiles_per_seq=seq // tm),
        grid=(t // tm,),
        in_specs=in_specs,
        out_specs=[pl.BlockSpec((tm, w), row) for w, _ in widths],
        out_shape=[jax.ShapeDtypeStruct((t, w), d) for w, d in widths],
        scratch_shapes=[
            pltpu.VMEM((tm, D_MODEL), BF16),
            pltpu.VMEM((NBLK // V7X_LANES, HALO + tm, V7X_LANES), F32),
            pltpu.VMEM((n_hist, HALO, V7X_LANES), F32),
            pltpu.VMEM((tm, NBLK), F32),
        ],
        compiler_params=pltpu.CompilerParams(
            dimension_semantics=("arbitrary",),
            vmem_limit_bytes=V7X_VMEM_LIMIT_BYTES),
        name="in_proj_ln" if apply_ln else "in_proj",
    )(*args)


def _split3_by_lane_group(v, lane):
    hi = v.astype(BF16)
    r1 = v - hi.astype(F32)
    mid = r1.astype(BF16)
    lo = (r1 - mid.astype(F32)).astype(BF16)
    return jnp.where(lane < SSM_HEADS, hi, jnp.where(lane < 2 * SSM_HEADS, mid, lo))


def _ssd_steps(xbc_ref, dt_ref, dtb_ref, alog_ref, dskip_ref, expand_ref, tril_ref,
               state_ref, ys_ref, slot, tq):
    q = SSM_CHUNK
    lane = lax.broadcasted_iota(jnp.int32, (q, DT_W), 1)
    causal = (lax.broadcasted_iota(jnp.int32, (q, q), 0)
              >= lax.broadcasted_iota(jnp.int32, (q, q), 1))
    lane_pair = lax.broadcasted_iota(jnp.int32, (q, 2 * SSM_HEAD_DIM), 1)
    steps = []
    for r0 in range(0, tq, q):
        rows = slice(r0, r0 + q)
        ck = {}

        def prologue(rows=rows, ck=ck):
            a_neg = -jnp.exp(alog_ref[...])
            xdt = dt_ref[rows, :] + dtb_ref[...]
            dt = jnp.maximum(xdt, 0.0) + jnp.log1p(jnp.exp(-jnp.abs(xdt)))
            a = dt * a_neg
            a_hi = a.astype(BF16)
            a_r1 = a - a_hi.astype(F32)
            a_mid = a_r1.astype(BF16)
            a_lo = (a_r1 - a_mid.astype(F32)).astype(BF16)
            cs = jnp.dot(tril_ref[...], jnp.concatenate([a_hi, a_mid, a_lo], axis=1),
                         preferred_element_type=F32)
            a_cum = cs[:, 0:DT_W] + cs[:, DT_W:2 * DT_W] + cs[:, 2 * DT_W:3 * DT_W]
            a2 = a_cum * LOG2E
            ea = jnp.exp2(a2)
            wst = dt * jnp.exp2(a2[q - 1:q, :] - a2)
            pieces = jnp.concatenate([_split3_by_lane_group(wst, lane),
                                      _split3_by_lane_group(ea, lane)], axis=0)
            expd = jnp.dot(pieces, expand_ref[:, 0:D_SSM], preferred_element_type=F32)
            ck["a2"] = a2
            ck["wst_x"] = expd[0:q, :]
            ck["ea_x"] = expd[q:2 * q, :]
            ck["row2_t"] = (a2 - jnp.log(dt) * LOG2E).T
            for g in range(SSM_GROUPS):
                b_g = xbc_ref[rows, D_SSM + g * SSM_STATE:D_SSM + (g + 1) * SSM_STATE]
                c_g = xbc_ref[rows, D_SSM + D_BC + g * SSM_STATE:D_SSM + D_BC + (g + 1) * SSM_STATE]
                ck["cb", g] = lax.dot_general(c_g, b_g, (((1,), (1,)), ((), ())),
                                              preferred_element_type=F32)
                ck["y_off", g] = jnp.dot(c_g, state_ref[g].astype(BF16), preferred_element_type=F32)

        steps.append(prologue)
        for g in range(SSM_GROUPS):
            def group(g=g, rows=rows, ck=ck):
                a2, wst_x, ea_x, row2_t = ck["a2"], ck["wst_x"], ck["ea_x"], ck["row2_t"]
                gcols = slice(g * GROUP_W, (g + 1) * GROUP_W)
                b_g = xbc_ref[rows, D_SSM + g * SSM_STATE:D_SSM + (g + 1) * SSM_STATE]
                cb = ck["cb", g]
                st = state_ref[g]
                x_gb = xbc_ref[rows, gcols]
                x_g = x_gb.astype(F32)
                y_off = ck["y_off", g] * ea_x[:, gcols]
                for j in range(HEADS_PER_GROUP // 2):
                    l_pair = []
                    for hh in range(2):
                        hd = g * HEADS_PER_GROUP + 2 * j + hh
                        seg = a2[:, hd:hd + 1] - row2_t[hd:hd + 1, :]
                        dec = jnp.exp2(jnp.where(causal, seg, -jnp.inf))
                        l_pair.append((cb * dec).astype(BF16))
                    pc = slice(j * 2 * SSM_HEAD_DIM, (j + 1) * 2 * SSM_HEAD_DIM)
                    x_pb = x_gb[:, pc]
                    zero = jnp.zeros_like(x_pb)
                    w_p = jnp.concatenate([jnp.where(lane_pair < SSM_HEAD_DIM, x_pb, zero),
                                           jnp.where(lane_pair >= SSM_HEAD_DIM, x_pb, zero)], axis=0)
                    y_diag = jnp.dot(jnp.concatenate(l_pair, axis=1), w_p, preferred_element_type=F32)
                    ac = slice(g * GROUP_W + j * 2 * SSM_HEAD_DIM,
                               g * GROUP_W + (j + 1) * 2 * SSM_HEAD_DIM)
                    ys_ref[slot, rows, ac] = y_diag + y_off[:, pc] + x_g[:, pc] * dskip_ref[:, ac]
                xw = (x_g * wst_x[:, gcols]).astype(BF16)
                new = lax.dot_general(b_g, xw, (((0,), (0,)), ((), ())), preferred_element_type=F32)
                state_ref[g] = st * ea_x[q - 1:q, gcols] + new

            steps.append(group)
    return steps


def _proj_steps(act_ref, h_ref, ng_ref, wa_ref, ws_ref, wo_ref, lng_ref, lnb_ref, out_ref,
                ys_ref, slot, ya_ref, ysp_ref, hn_ref, mixed_ref):
    n_blk = D_MODEL // KBLK
    steps = []
    for nb in range(n_blk):
        def branch_a(nb=nb):
            cols = slice(nb * KBLK, (nb + 1) * KBLK)
            ya_ref[:, cols] = jnp.dot(act_ref[:, ACT_YA:ACT_YA + D_CONV_BR], wa_ref[:, cols],
                                      preferred_element_type=F32)
        steps.append(branch_a)
    for g in range(SSM_GROUPS):
        def norm_s(g=g):
            gcols = slice(g * GROUP_W, (g + 1) * GROUP_W)
            hg = ys_ref[slot, :, gcols] * act_ref[:, ACT_ZS + g * GROUP_W:ACT_ZS + (g + 1) * GROUP_W].astype(F32)
            ms = jnp.mean(hg * hg, axis=-1, keepdims=True)
            hn_ref[:, gcols] = (hg * lax.rsqrt(ms + RMS_EPS) * ng_ref[:, gcols]).astype(BF16)
        steps.append(norm_s)
    for nb in range(n_blk):
        def branch_s(nb=nb):
            cols = slice(nb * KBLK, (nb + 1) * KBLK)
            ysp_ref[:, cols] = jnp.dot(hn_ref[:, 0:D_SSM], ws_ref[:, cols], preferred_element_type=F32)
        steps.append(branch_s)
    for nb in range(n_blk):
        def merge(nb=nb):
            cols = slice(nb * KBLK, (nb + 1) * KBLK)
            sig_a = act_ref[:, ACT_GA + nb * KBLK:ACT_GA + (nb + 1) * KBLK].astype(F32)
            sig_s = act_ref[:, ACT_GS + nb * KBLK:ACT_GS + (nb + 1) * KBLK].astype(F32)
            mixed_ref[:, cols] = (sig_a * ya_ref[:, cols] + sig_s * ysp_ref[:, cols]).astype(BF16)
        steps.append(merge)
    for nb in range(n_blk):
        def project(nb=nb):
            cols = slice(nb * KBLK, (nb + 1) * KBLK)
            out_ref[:, cols] = jnp.dot(mixed_ref[:, 0:D_MODEL], wo_ref[:, cols], preferred_element_type=F32)
        steps.append(project)

    def deepnorm():
        out_ref[...] = _layer_norm(DEEPNORM_ALPHA * h_ref[...] + out_ref[...], lng_ref[...], lnb_ref[...])

    steps.append(deepnorm)
    return steps


def _interleave(a, b):
    if len(a) < len(b):
        a, b = b, a
    done = 0
    for i, fa in enumerate(a):
        fa()
        due = ((i + 1) * len(b)) // len(a)
        for fb in b[done:due]:
            fb()
        done = due


def _mixer_kernel(xbc_ref, dt_ref, act_ref, h_ref, dtb_ref, alog_ref,
                  dskip_ref, ng_ref, wa_ref, ws_ref, wo_ref, lng_ref, lnb_ref,
                  expand_ref, tril_ref,
                  out_ref,
                  state_ref, ys_ref, ya_ref, ysp_ref, hn_ref, mixed_ref, *, tq, n_tiles):
    c = pl.program_id(1)
    slot = c % 2

    def ssd():
        return _ssd_steps(xbc_ref, dt_ref, dtb_ref, alog_ref, dskip_ref, expand_ref, tril_ref,
                          state_ref, ys_ref, slot, tq)

    def proj():
        return _proj_steps(act_ref, h_ref, ng_ref, wa_ref, ws_ref, wo_ref, lng_ref, lnb_ref, out_ref,
                           ys_ref, 1 - slot, ya_ref, ysp_ref, hn_ref, mixed_ref)

    @pl.when(c == 0)
    def _():
        state_ref[...] = jnp.zeros_like(state_ref)
        for f in ssd():
            f()

    @pl.when(jnp.logical_and(c > 0, c < n_tiles))
    def _():
        _interleave(ssd(), proj())

    @pl.when(c == n_tiles)
    def _():
        for f in proj():
            f()


def _mixer(xbc, dt, act, h, p, *, bsz, seq, tq):
    t = bsz * seq
    nt = seq // tq
    cur = lambda b, c: (b * nt + jnp.minimum(c, nt - 1), 0)
    prev = lambda b, c: (b * nt + jnp.maximum(c - 1, 0), 0)
    const2 = lambda b, c: (0, 0)

    def full(a):
        return pl.BlockSpec(a.shape, const2, pipeline_mode=pl.Buffered(1))

    params = [p["dt_bias"], p["a_log"], p["d_skip"],
              p["norm_s_g"], p["w_a_out"], p["w_s_out"], p["w_o"], p["ln_g"], p["ln_b"],
              p["expand"], p["tril"]]
    return pl.pallas_call(
        functools.partial(_mixer_kernel, tq=tq, n_tiles=nt),
        grid=(bsz, nt + 1),
        in_specs=[pl.BlockSpec((tq, D_XBC + WIDTH_PAD), cur), pl.BlockSpec((tq, DT_W), cur),
                  pl.BlockSpec((tq, N_ACT + WIDTH_PAD), prev), pl.BlockSpec((tq, D_MODEL), prev)]
                 + [full(a) for a in params],
        out_specs=pl.BlockSpec((tq, D_MODEL), prev),
        out_shape=jax.ShapeDtypeStruct((t, D_MODEL), F32),
        scratch_shapes=[
            pltpu.VMEM((SSM_GROUPS, SSM_STATE, GROUP_W), F32),
            pltpu.VMEM((2, tq, D_SSM), F32),
            pltpu.VMEM((tq, D_MODEL), F32),
            pltpu.VMEM((tq, D_MODEL), F32),
            pltpu.VMEM((tq, D_SSM + WIDTH_PAD), BF16),
            pltpu.VMEM((tq, D_MODEL + WIDTH_PAD), BF16),
        ],
        compiler_params=pltpu.CompilerParams(
            dimension_semantics=("arbitrary", "arbitrary"),
            vmem_limit_bytes=V7X_VMEM_LIMIT_BYTES),
        name="mixer",
    )(xbc, dt, act, h, *params)


def _pad_heads(v):
    return jnp.pad(jnp.tile(v, DT_REP), (0, DT_W - DT_REP * SSM_HEADS)).reshape(1, DT_W)


def _pad_cols(w):
    return jnp.pad(w, ((0, 0), (0, WIDTH_PAD)))


def kernel(x, ln_in_g, ln_in_b, w_in, conv_a_w, w_a_out, conv_s_w, conv_s_b, dt_bias, a_log, d_skip,
           norm_s_g, w_s_out, w_o, ln_g, ln_b):
    bsz, seq, _ = x.shape
    t = bsz * seq
    tq = 2 * SSM_CHUNK
    tm = min(512, seq)

    head_of_row = jnp.arange(DT_W) % SSM_HEADS
    valid_row = jnp.arange(DT_W) < DT_REP * SSM_HEADS
    head_of_col = jnp.arange(D_SSM) // SSM_HEAD_DIM
    expand = _pad_cols(((head_of_row[:, None] == head_of_col[None, :]) & valid_row[:, None]).astype(BF16))
    tril = (jnp.arange(SSM_CHUNK)[:, None] >= jnp.arange(SSM_CHUNK)[None, :]).astype(BF16)

    h = x.reshape(t, D_MODEL)
    for i in range(DEPTH):
        w = w_in[i]
        w_dt = jnp.pad(jnp.tile(w[:, DT_COL_IN_W:DT_COL_IN_W + SSM_HEADS], (1, DT_REP)),
                       ((0, 0), (0, DT_W - DT_REP * SSM_HEADS)))
        w_main = jnp.concatenate([w[:, :DT_COL_IN_W], w[:, DT_COL_IN_W + SSM_HEADS:], w_dt],
                                 axis=1).astype(BF16)
        conv_params = (conv_a_w[i], conv_s_w[i], conv_s_b[i].reshape(1, -1))
        if i == 0:
            act, xbc, dt, h = _in_proj(h, w_main, *conv_params, ln_in_g.reshape(1, -1),
                                       ln_in_b.reshape(1, -1), tm=tm, seq=seq)
        else:
            act, xbc, dt = _in_proj(h, w_main, *conv_params, tm=tm, seq=seq)
        p = {
            "dt_bias": _pad_heads(dt_bias[i]), "a_log": _pad_heads(a_log[i]),
            "d_skip": jnp.repeat(d_skip[i], SSM_HEAD_DIM).reshape(1, -1),
            "norm_s_g": norm_s_g[i].reshape(1, -1),
            "w_a_out": _pad_cols(w_a_out[i].astype(BF16)), "w_s_out": _pad_cols(w_s_out[i].astype(BF16)),
            "w_o": _pad_cols(w_o[i].astype(BF16)),
            "ln_g": ln_g[i].reshape(1, -1), "ln_b": ln_b[i].reshape(1, -1),
            "expand": expand, "tril": tril,
        }
        h = _mixer(xbc, dt, act, h, p, bsz=bsz, seq=seq, tq=tq)
    return h.reshape(bsz, seq, D_MODEL)
```

```python
import functools

import jax
import jax.numpy as jnp
from jax import lax
from jax.experimental import pallas as pl
from jax.experimental.pallas import tpu as pltpu

D_MODEL = 1024
DEPTH = 2
D_CONV_BR = D_MODEL
CONV_A_WIDTH = 3
D_SSM = 2 * D_MODEL
SSM_HEAD_DIM = 64
SSM_HEADS = D_SSM // SSM_HEAD_DIM
SSM_GROUPS = 4
HEADS_PER_GROUP = SSM_HEADS // SSM_GROUPS
SSM_STATE = 128
SSM_CONV_WIDTH = 4
SSM_CHUNK = 128
D_BC = SSM_GROUPS * SSM_STATE
D_XBC = D_SSM + 2 * D_BC
GROUP_W = HEADS_PER_GROUP * SSM_HEAD_DIM
DEEPNORM_ALPHA = (2 * DEPTH) ** 0.25
LN_EPS = 1e-5
RMS_EPS = 1e-5
LOG2E = 1.4426950408889634

OFF_U = 0
OFF_BG = OFF_U + D_CONV_BR
OFF_CG = OFF_BG + D_CONV_BR
OFF_ZA = OFF_CG + D_CONV_BR
OFF_ZS = OFF_ZA + D_CONV_BR
OFF_XBC = OFF_ZS + D_SSM
OFF_GA = OFF_XBC + D_XBC
OFF_GS = OFF_GA + D_MODEL
N_MAIN = OFF_GS + D_MODEL
DT_COL_IN_W = OFF_GA

ACT_YA = 0
ACT_ZS = ACT_YA + D_CONV_BR
ACT_GA = ACT_ZS + D_SSM
ACT_GS = ACT_GA + D_MODEL
N_ACT = ACT_GS + D_MODEL

V7X_LANES = 128
V7X_SUBLANES = 8
V7X_VMEM_LIMIT_BYTES = 56 * 1024 * 1024

WIDTH_PAD = V7X_LANES
DT_W = V7X_LANES
DT_REP = 3
NBLK = 512
KBLK = 256
HALO = V7X_SUBLANES

F32 = jnp.float32
BF16 = jnp.bfloat16


def _layer_norm(x, g, b):
    mu = jnp.mean(x, axis=-1, keepdims=True)
    xc = x - mu
    var = jnp.mean(xc * xc, axis=-1, keepdims=True)
    return xc * lax.rsqrt(var + LN_EPS) * g + b


def _sigmoid_of_half(hx):
    return 0.5 * jnp.tanh(hx) + 0.5


def _silu_of_half(hx):
    return hx * (1.0 + jnp.tanh(hx))


def _conv_fill(v, slab_ref, buf, hist_ref, hist0, tm):
    for t in range(NBLK // V7X_LANES):
        slab_ref[buf, t, 0:HALO, :] = hist_ref[hist0 + t]
        slab_ref[buf, t, HALO:HALO + tm, :] = v[:, t * V7X_LANES:(t + 1) * V7X_LANES]
        hist_ref[hist0 + t] = slab_ref[buf, t, tm:tm + HALO, :]


def _conv_taps(slab_ref, buf, w_ref, b_ref, col0, width, tm):
    outs = []
    for t in range(NBLK // V7X_LANES):
        cols = slice(col0 + t * V7X_LANES, col0 + (t + 1) * V7X_LANES)
        acc = None
        for k in range(width):
            sh = width - 1 - k
            term = w_ref[k:k + 1, cols] * slab_ref[buf, t, HALO - sh:HALO - sh + tm, :]
            acc = term if acc is None else acc + term
        if b_ref is not None:
            acc = acc + b_ref[:, cols]
        outs.append(acc)
    return jnp.concatenate(outs, axis=1)


def _in_proj_kernel(*refs, apply_ln, tm, tiles_per_seq):
    if apply_ln:
        (x_ref, g_ref, b_ref, w_ref, caw_ref, csw_ref, csb_ref,
         act_ref, xbc_ref, dt_ref, h_ref, hb_ref, slab_ref, hist_ref, tmp_ref) = refs
    else:
        (x_ref, w_ref, caw_ref, csw_ref, csb_ref,
         act_ref, xbc_ref, dt_ref, hb_ref, slab_ref, hist_ref, tmp_ref) = refs

    @pl.when(pl.program_id(0) % tiles_per_seq == 0)
    def _():
        hist_ref[...] = jnp.zeros_like(hist_ref)

    x = x_ref[...]
    if apply_ln:
        x = _layer_norm(x, g_ref[...], b_ref[...])
        h_ref[...] = x
    hb_ref[...] = x.astype(BF16)
    dt_ref[...] = jnp.dot(hb_ref[...], w_ref[:, N_MAIN:N_MAIN + DT_W], preferred_element_type=F32)
    act_ref[:, N_ACT:N_ACT + WIDTH_PAD] = jnp.zeros((tm, WIDTH_PAD), BF16)
    xbc_ref[:, D_XBC:D_XBC + WIDTH_PAD] = jnp.zeros((tm, WIDTH_PAD), BF16)

    def mm(c0):
        return jnp.dot(hb_ref[...], w_ref[:, c0:c0 + NBLK], preferred_element_type=F32)

    hist_x0 = D_CONV_BR // V7X_LANES
    for c0 in range(0, D_CONV_BR, NBLK):
        tmp_ref[...] = mm(OFF_U + c0)
        _conv_fill(mm(OFF_CG + c0) * tmp_ref[...], slab_ref, 0, hist_ref, c0 // V7X_LANES, tm)
        conv_a = _conv_taps(slab_ref, 0, caw_ref, None, c0, CONV_A_WIDTH, tm)
        tmp_ref[...] = _silu_of_half(mm(OFF_ZA + c0)) * conv_a
        act_ref[:, ACT_YA + c0:ACT_YA + c0 + NBLK] = (mm(OFF_BG + c0) * tmp_ref[...]).astype(BF16)
    for c0 in range(0, D_SSM, NBLK):
        act_ref[:, ACT_ZS + c0:ACT_ZS + c0 + NBLK] = _silu_of_half(mm(OFF_ZS + c0)).astype(BF16)
    def finish_xbc(c0, buf):
        conv_s = _conv_taps(slab_ref, buf, csw_ref, csb_ref, c0, SSM_CONV_WIDTH, tm)
        xbc_ref[:, c0:c0 + NBLK] = _silu_of_half(conv_s).astype(BF16)

    for k, c0 in enumerate(range(0, D_XBC, NBLK)):
        _conv_fill(mm(OFF_XBC + c0), slab_ref, k % 2, hist_ref, hist_x0 + c0 // V7X_LANES, tm)
        if k > 0:
            finish_xbc(c0 - NBLK, (k - 1) % 2)
    finish_xbc(D_XBC - NBLK, (D_XBC // NBLK - 1) % 2)
    for c0 in range(0, 2 * D_MODEL, NBLK):
        act_ref[:, ACT_GA + c0:ACT_GA + c0 + NBLK] = _sigmoid_of_half(mm(OFF_GA + c0)).astype(BF16)


def _in_proj(x, w_main, conv_a_w, conv_s_w, conv_s_b, ln_g=None, ln_b=None, *, tm, seq):
    t = x.shape[0]
    apply_ln = ln_g is not None
    row = lambda i: (i, 0)
    const = lambda i: (0, 0)

    def resident(a):
        return pl.BlockSpec(a.shape, const, pipeline_mode=pl.Buffered(1))

    in_specs = [pl.BlockSpec((tm, D_MODEL), row)]
    args = [x]
    if apply_ln:
        in_specs += [resident(ln_g), resident(ln_b)]
        args += [ln_g, ln_b]
    params = [w_main, conv_a_w, conv_s_w, conv_s_b]
    in_specs += [resident(a) for a in params]
    args += params
    widths = [(N_ACT + WIDTH_PAD, BF16), (D_XBC + WIDTH_PAD, BF16), (DT_W, F32)]
    if apply_ln:
        widths.append((D_MODEL, F32))
    n_hist = (D_CONV_BR + D_XBC) // V7X_LANES
    return pl.pallas_call(
        functools.partial(_in_proj_kernel, apply_ln=apply_ln, tm=tm, tiles_per_seq=seq // tm),
        grid=(t // tm,),
        in_specs=in_specs,
        out_specs=[pl.BlockSpec((tm, w), row) for w, _ in widths],
        out_shape=[jax.ShapeDtypeStruct((t, w), d) for w, d in widths],
        scratch_shapes=[
            pltpu.VMEM((tm, D_MODEL), BF16),
            pltpu.VMEM((2, NBLK // V7X_LANES, HALO + tm, V7X_LANES), F32),
            pltpu.VMEM((n_hist, HALO, V7X_LANES), F32),
            pltpu.VMEM((tm, NBLK), F32),
        ],
        compiler_params=pltpu.CompilerParams(
            dimension_semantics=("arbitrary",),
            vmem_limit_bytes=V7X_VMEM_LIMIT_BYTES),
        name="in_proj_ln" if apply_ln else "in_proj",
    )(*args)


def _split3_by_lane_group(v, lane):
    hi = v.astype(BF16)
    r1 = v - hi.astype(F32)
    mid = r1.astype(BF16)
    lo = (r1 - mid.astype(F32)).astype(BF16)
    return jnp.where(lane < SSM_HEADS, hi, jnp.where(lane < 2 * SSM_HEADS, mid, lo))


def _ssd_steps(xbc_ref, dt_ref, dtb_ref, alog_ref, dskip_ref, expand_ref, tril_ref,
               state_ref, ys_ref, slot, tq):
    q = SSM_CHUNK
    lane = lax.broadcasted_iota(jnp.int32, (q, DT_W), 1)
    causal = (lax.broadcasted_iota(jnp.int32, (q, q), 0)
              >= lax.broadcasted_iota(jnp.int32, (q, q), 1))
    lane_pair = lax.broadcasted_iota(jnp.int32, (q, 2 * SSM_HEAD_DIM), 1)
    steps = []
    for r0 in range(0, tq, q):
        rows = slice(r0, r0 + q)
        ck = {}

        def prologue(rows=rows, ck=ck):
            a_neg = -jnp.exp(alog_ref[...])
            xdt = dt_ref[rows, :] + dtb_ref[...]
            dt = jnp.maximum(xdt, 0.0) + jnp.log1p(jnp.exp(-jnp.abs(xdt)))
            a = dt * a_neg
            a_hi = a.astype(BF16)
            a_r1 = a - a_hi.astype(F32)
            a_mid = a_r1.astype(BF16)
            a_lo = (a_r1 - a_mid.astype(F32)).astype(BF16)
            cs = jnp.dot(tril_ref[...], jnp.concatenate([a_hi, a_mid, a_lo], axis=1),
                         preferred_element_type=F32)
            a_cum = cs[:, 0:DT_W] + cs[:, DT_W:2 * DT_W] + cs[:, 2 * DT_W:3 * DT_W]
            a2 = a_cum * LOG2E
            ea = jnp.exp2(a2)
            wst = dt * jnp.exp2(a2[q - 1:q, :] - a2)
            pieces = jnp.concatenate([_split3_by_lane_group(wst, lane),
                                      _split3_by_lane_group(ea, lane)], axis=0)
            expd = jnp.dot(pieces, expand_ref[:, 0:D_SSM], preferred_element_type=F32)
            ck["a2"] = a2
            ck["wst_x"] = expd[0:q, :]
            ck["ea_x"] = expd[q:2 * q, :]
            ck["row2_t"] = (a2 - jnp.log(dt) * LOG2E).T
            for g in range(SSM_GROUPS):
                b_g = xbc_ref[rows, D_SSM + g * SSM_STATE:D_SSM + (g + 1) * SSM_STATE]
                c_g = xbc_ref[rows, D_SSM + D_BC + g * SSM_STATE:D_SSM + D_BC + (g + 1) * SSM_STATE]
                ck["cb", g] = lax.dot_general(c_g, b_g, (((1,), (1,)), ((), ())),
                                              preferred_element_type=F32)
                ck["y_off", g] = jnp.dot(c_g, state_ref[g].astype(BF16), preferred_element_type=F32)

        steps.append(prologue)
        for g in range(SSM_GROUPS):
            def group(g=g, rows=rows, ck=ck):
                a2, wst_x, ea_x, row2_t = ck["a2"], ck["wst_x"], ck["ea_x"], ck["row2_t"]
                gcols = slice(g * GROUP_W, (g + 1) * GROUP_W)
                b_g = xbc_ref[rows, D_SSM + g * SSM_STATE:D_SSM + (g + 1) * SSM_STATE]
                cb = ck["cb", g]
                st = state_ref[g]
                x_gb = xbc_ref[rows, gcols]
                x_g = x_gb.astype(F32)
                y_off = ck["y_off", g] * ea_x[:, gcols]
                for j in range(HEADS_PER_GROUP // 2):
                    l_pair = []
                    for hh in range(2):
                        hd = g * HEADS_PER_GROUP + 2 * j + hh
                        seg = a2[:, hd:hd + 1] - row2_t[hd:hd + 1, :]
                        dec = jnp.exp2(jnp.where(causal, seg, -jnp.inf))
                        l_pair.append((cb * dec).astype(BF16))
                    pc = slice(j * 2 * SSM_HEAD_DIM, (j + 1) * 2 * SSM_HEAD_DIM)
                    x_pb = x_gb[:, pc]
                    zero = jnp.zeros_like(x_pb)
                    w_p = jnp.concatenate([jnp.where(lane_pair < SSM_HEAD_DIM, x_pb, zero),
                                           jnp.where(lane_pair >= SSM_HEAD_DIM, x_pb, zero)], axis=0)
                    y_diag = jnp.dot(jnp.concatenate(l_pair, axis=1), w_p, preferred_element_type=F32)
                    ac = slice(g * GROUP_W + j * 2 * SSM_HEAD_DIM,
                               g * GROUP_W + (j + 1) * 2 * SSM_HEAD_DIM)
                    ys_ref[slot, rows, ac] = y_diag + y_off[:, pc] + x_g[:, pc] * dskip_ref[:, ac]
                xw = (x_g * wst_x[:, gcols]).astype(BF16)
                new = lax.dot_general(b_g, xw, (((0,), (0,)), ((), ())), preferred_element_type=F32)
                state_ref[g] = st * ea_x[q - 1:q, gcols] + new

            steps.append(group)
    return steps


def _proj_steps(act_ref, h_ref, ng_ref, wa_ref, ws_ref, wo_ref, lng_ref, lnb_ref, out_ref,
                ys_ref, slot, ya_ref, ysp_ref, hn_ref, mixed_ref):
    n_blk = D_MODEL // KBLK
    steps = []
    for nb in range(n_blk):
        def branch_a(nb=nb):
            cols = slice(nb * KBLK, (nb + 1) * KBLK)
            ya_ref[:, cols] = jnp.dot(act_ref[:, ACT_YA:ACT_YA + D_CONV_BR], wa_ref[:, cols],
                                      preferred_element_type=F32)
        steps.append(branch_a)
    for g in range(SSM_GROUPS):
        def norm_s(g=g):
            gcols = slice(g * GROUP_W, (g + 1) * GROUP_W)
            hg = ys_ref[slot, :, gcols] * act_ref[:, ACT_ZS + g * GROUP_W:ACT_ZS + (g + 1) * GROUP_W].astype(F32)
            ms = jnp.mean(hg * hg, axis=-1, keepdims=True)
            hn_ref[:, gcols] = (hg * lax.rsqrt(ms + RMS_EPS) * ng_ref[:, gcols]).astype(BF16)
        steps.append(norm_s)
    for nb in range(n_blk):
        def branch_s(nb=nb):
            cols = slice(nb * KBLK, (nb + 1) * KBLK)
            ysp_ref[:, cols] = jnp.dot(hn_ref[:, 0:D_SSM], ws_ref[:, cols], preferred_element_type=F32)
        steps.append(branch_s)
    for nb in range(n_blk):
        def merge(nb=nb):
            cols = slice(nb * KBLK, (nb + 1) * KBLK)
            sig_a = act_ref[:, ACT_GA + nb * KBLK:ACT_GA + (nb + 1) * KBLK].astype(F32)
            sig_s = act_ref[:, ACT_GS + nb * KBLK:ACT_GS + (nb + 1) * KBLK].astype(F32)
            mixed_ref[:, cols] = (sig_a * ya_ref[:, cols] + sig_s * ysp_ref[:, cols]).astype(BF16)
        steps.append(merge)
    for nb in range(n_blk):
        def project(nb=nb):
            cols = slice(nb * KBLK, (nb + 1) * KBLK)
            out_ref[:, cols] = jnp.dot(mixed_ref[:, 0:D_MODEL], wo_ref[:, cols], preferred_element_type=F32)
        steps.append(project)

    def deepnorm():
        out_ref[...] = _layer_norm(DEEPNORM_ALPHA * h_ref[...] + out_ref[...], lng_ref[...], lnb_ref[...])

    steps.append(deepnorm)
    return steps


def _interleave(a, b):
    if len(a) < len(b):
        a, b = b, a
    done = 0
    for i, fa in enumerate(a):
        fa()
        due = ((i + 1) * len(b)) // len(a)
        for fb in b[done:due]:
            fb()
        done = due


def _mixer_kernel(xbc_ref, dt_ref, act_ref, h_ref, dtb_ref, alog_ref,
                  dskip_ref, ng_ref, wa_ref, ws_ref, wo_ref, lng_ref, lnb_ref,
                  expand_ref, tril_ref,
                  out_ref,
                  state_ref, ys_ref, ya_ref, ysp_ref, hn_ref, mixed_ref, *, tq, n_tiles):
    c = pl.program_id(1)
    slot = c % 2

    def ssd():
        return _ssd_steps(xbc_ref, dt_ref, dtb_ref, alog_ref, dskip_ref, expand_ref, tril_ref,
                          state_ref, ys_ref, slot, tq)

    def proj():
        return _proj_steps(act_ref, h_ref, ng_ref, wa_ref, ws_ref, wo_ref, lng_ref, lnb_ref, out_ref,
                           ys_ref, 1 - slot, ya_ref, ysp_ref, hn_ref, mixed_ref)

    @pl.when(c == 0)
    def _():
        state_ref[...] = jnp.zeros_like(state_ref)
        for f in ssd():
            f()

    @pl.when(jnp.logical_and(c > 0, c < n_tiles))
    def _():
        _interleave(ssd(), proj())

    @pl.when(c == n_tiles)
    def _():
        for f in proj():
            f()


def _mixer(xbc, dt, act, h, p, *, bsz, seq, tq):
    t = bsz * seq
    nt = seq // tq
    cur = lambda b, c: (b * nt + jnp.minimum(c, nt - 1), 0)
    prev = lambda b, c: (b * nt + jnp.maximum(c - 1, 0), 0)
    const2 = lambda b, c: (0, 0)

    def full(a):
        return pl.BlockSpec(a.shape, const2, pipeline_mode=pl.Buffered(1))

    params = [p["dt_bias"], p["a_log"], p["d_skip"],
              p["norm_s_g"], p["w_a_out"], p["w_s_out"], p["w_o"], p["ln_g"], p["ln_b"],
              p["expand"], p["tril"]]
    return pl.pallas_call(
        functools.partial(_mixer_kernel, tq=tq, n_tiles=nt),
        grid=(bsz, nt + 1),
        in_specs=[pl.BlockSpec((tq, D_XBC + WIDTH_PAD), cur), pl.BlockSpec((tq, DT_W), cur),
                  pl.BlockSpec((tq, N_ACT + WIDTH_PAD), prev), pl.BlockSpec((tq, D_MODEL), prev)]
                 + [full(a) for a in params],
        out_specs=pl.BlockSpec((tq, D_MODEL), prev),
        out_shape=jax.ShapeDtypeStruct((t, D_MODEL), F32),
        scratch_shapes=[
            pltpu.VMEM((SSM_GROUPS, SSM_STATE, GROUP_W), F32),
            pltpu.VMEM((2, tq, D_SSM), F32),
            pltpu.VMEM((tq, D_MODEL), F32),
            pltpu.VMEM((tq, D_MODEL), F32),
            pltpu.VMEM((tq, D_SSM + WIDTH_PAD), BF16),
            pltpu.VMEM((tq, D_MODEL + WIDTH_PAD), BF16),
        ],
        compiler_params=pltpu.CompilerParams(
            dimension_semantics=("arbitrary", "arbitrary"),
            vmem_limit_bytes=V7X_VMEM_LIMIT_BYTES),
        name="mixer",
    )(xbc, dt, act, h, *params)


def _pad_heads(v):
    return jnp.pad(jnp.tile(v, DT_REP), (0, DT_W - DT_REP * SSM_HEADS)).reshape(1, DT_W)


def _pad_cols(w):
    return jnp.pad(w, ((0, 0), (0, WIDTH_PAD)))


def kernel(x, ln_in_g, ln_in_b, w_in, conv_a_w, w_a_out, conv_s_w, conv_s_b, dt_bias, a_log, d_skip,
           norm_s_g, w_s_out, w_o, ln_g, ln_b):
    bsz, seq, _ = x.shape
    t = bsz * seq
    tq = 2 * SSM_CHUNK
    tm = min(512, seq)

    head_of_row = jnp.arange(DT_W) % SSM_HEADS
    valid_row = jnp.arange(DT_W) < DT_REP * SSM_HEADS
    head_of_col = jnp.arange(D_SSM) // SSM_HEAD_DIM
    expand = _pad_cols(((head_of_row[:, None] == head_of_col[None, :]) & valid_row[:, None]).astype(BF16))
    tril = (jnp.arange(SSM_CHUNK)[:, None] >= jnp.arange(SSM_CHUNK)[None, :]).astype(BF16)

    col = jnp.arange(N_MAIN + DT_W)
    halved = ((col >= OFF_ZA) & (col < OFF_XBC)) | ((col >= OFF_GA) & (col < N_MAIN))
    col_scale = jnp.where(halved, 0.5, 1.0).astype(F32)[None, :]

    h = x.reshape(t, D_MODEL)
    for i in range(DEPTH):
        w = w_in[i]
        w_dt = jnp.pad(jnp.tile(w[:, DT_COL_IN_W:DT_COL_IN_W + SSM_HEADS], (1, DT_REP)),
                       ((0, 0), (0, DT_W - DT_REP * SSM_HEADS)))
        w_main = jnp.concatenate([w[:, :DT_COL_IN_W], w[:, DT_COL_IN_W + SSM_HEADS:], w_dt], axis=1)
        w_main = (w_main * col_scale).astype(BF16)
        conv_params = (conv_a_w[i], 0.5 * conv_s_w[i], 0.5 * conv_s_b[i].reshape(1, -1))
        if i == 0:
            act, xbc, dt, h = _in_proj(h, w_main, *conv_params, ln_in_g.reshape(1, -1),
                                       ln_in_b.reshape(1, -1), tm=tm, seq=seq)
        else:
            act, xbc, dt = _in_proj(h, w_main, *conv_params, tm=tm, seq=seq)
        p = {
            "dt_bias": _pad_heads(dt_bias[i]), "a_log": _pad_heads(a_log[i]),
            "d_skip": jnp.repeat(d_skip[i], SSM_HEAD_DIM).reshape(1, -1),
            "norm_s_g": norm_s_g[i].reshape(1, -1),
            "w_a_out": _pad_cols(w_a_out[i].astype(BF16)), "w_s_out": _pad_cols(w_s_out[i].astype(BF16)),
            "w_o": _pad_cols(w_o[i].astype(BF16)),
            "ln_g": ln_g[i].reshape(1, -1), "ln_b": ln_b[i].reshape(1, -1),
            "expand": expand, "tril": tril,
        }
        h = _mixer(xbc, dt, act, h, p, bsz=bsz, seq=seq, tq=tq)
    return h.reshape(bsz, seq, D_MODEL)
```

```python
import functools

import jax
import jax.numpy as jnp
from jax import lax
from jax.experimental import pallas as pl
from jax.experimental.pallas import tpu as pltpu

D_MODEL = 1024
DEPTH = 2
D_CONV_BR = D_MODEL
CONV_A_WIDTH = 3
D_SSM = 2 * D_MODEL
SSM_HEAD_DIM = 64
SSM_HEADS = D_SSM // SSM_HEAD_DIM
SSM_GROUPS = 4
HEADS_PER_GROUP = SSM_HEADS // SSM_GROUPS
SSM_STATE = 128
SSM_CONV_WIDTH = 4
SSM_CHUNK = 128
D_BC = SSM_GROUPS * SSM_STATE
D_XBC = D_SSM + 2 * D_BC
GROUP_W = HEADS_PER_GROUP * SSM_HEAD_DIM
DEEPNORM_ALPHA = (2 * DEPTH) ** 0.25
LN_EPS = 1e-5
RMS_EPS = 1e-5
LOG2E = 1.4426950408889634

OFF_U = 0
OFF_BG = OFF_U + D_CONV_BR
OFF_CG = OFF_BG + D_CONV_BR
OFF_ZA = OFF_CG + D_CONV_BR
OFF_ZS = OFF_ZA + D_CONV_BR
OFF_XBC = OFF_ZS + D_SSM
OFF_GA = OFF_XBC + D_XBC
OFF_GS = OFF_GA + D_MODEL
N_MAIN = OFF_GS + D_MODEL
DT_COL_IN_W = OFF_GA

ACT_YA = 0
ACT_ZS = ACT_YA + D_CONV_BR
ACT_GA = ACT_ZS + D_SSM
ACT_GS = ACT_GA + D_MODEL
N_ACT = ACT_GS + D_MODEL

V7X_LANES = 128
V7X_SUBLANES = 8
V7X_VMEM_LIMIT_BYTES = 56 * 1024 * 1024

WIDTH_PAD = V7X_LANES
DT_W = V7X_LANES
DT_REP = 3
NBLK = 512
KBLK = 256
HALO = V7X_SUBLANES

F32 = jnp.float32
BF16 = jnp.bfloat16


def _layer_norm(x, g, b):
    mu = jnp.mean(x, axis=-1, keepdims=True)
    xc = x - mu
    var = jnp.mean(xc * xc, axis=-1, keepdims=True)
    return xc * lax.rsqrt(var + LN_EPS) * g + b


def _sigmoid_of_half(hx):
    return 0.5 * jnp.tanh(hx) + 0.5


def _silu_of_half(hx):
    return hx * (1.0 + jnp.tanh(hx))


def _conv_fill(v, slab_ref, buf, hist_ref, hist0, tm):
    for t in range(NBLK // V7X_LANES):
        slab_ref[buf, t, 0:HALO, :] = hist_ref[hist0 + t]
        slab_ref[buf, t, HALO:HALO + tm, :] = v[:, t * V7X_LANES:(t + 1) * V7X_LANES]
        hist_ref[hist0 + t] = slab_ref[buf, t, tm:tm + HALO, :]


def _conv_taps(slab_ref, buf, w_ref, b_ref, col0, width, tm):
    outs = []
    for t in range(NBLK // V7X_LANES):
        cols = slice(col0 + t * V7X_LANES, col0 + (t + 1) * V7X_LANES)
        acc = None
        for k in range(width):
            sh = width - 1 - k
            term = w_ref[k:k + 1, cols] * slab_ref[buf, t, HALO - sh:HALO - sh + tm, :]
            acc = term if acc is None else acc + term
        if b_ref is not None:
            acc = acc + b_ref[:, cols]
        outs.append(acc)
    return jnp.concatenate(outs, axis=1)


def _in_proj_kernel(*refs, apply_ln, tm, tiles_per_seq):
    if apply_ln:
        (x_ref, g_ref, b_ref, w_ref, caw_ref, csw_ref, csb_ref,
         act_ref, xbc_ref, dt_ref, h_ref, hb_ref, slab_ref, hist_ref, tmp_ref) = refs
    else:
        (x_ref, w_ref, caw_ref, csw_ref, csb_ref,
         act_ref, xbc_ref, dt_ref, hb_ref, slab_ref, hist_ref, tmp_ref) = refs

    @pl.when(pl.program_id(0) % tiles_per_seq == 0)
    def _():
        hist_ref[...] = jnp.zeros_like(hist_ref)

    x = x_ref[...]
    if apply_ln:
        x = _layer_norm(x, g_ref[...], b_ref[...])
        h_ref[...] = x
    hb_ref[...] = x.astype(BF16)
    dt_ref[...] = jnp.dot(hb_ref[...], w_ref[:, N_MAIN:N_MAIN + DT_W], preferred_element_type=F32)
    act_ref[:, N_ACT:N_ACT + WIDTH_PAD] = jnp.zeros((tm, WIDTH_PAD), BF16)
    xbc_ref[:, D_XBC:D_XBC + WIDTH_PAD] = jnp.zeros((tm, WIDTH_PAD), BF16)

    def mm(c0):
        return jnp.dot(hb_ref[...], w_ref[:, c0:c0 + NBLK], preferred_element_type=F32)

    hist_x0 = D_CONV_BR // V7X_LANES
    for c0 in range(0, D_CONV_BR, NBLK):
        tmp_ref[...] = mm(OFF_U + c0)
        _conv_fill(mm(OFF_CG + c0) * tmp_ref[...], slab_ref, 0, hist_ref, c0 // V7X_LANES, tm)
        conv_a = _conv_taps(slab_ref, 0, caw_ref, None, c0, CONV_A_WIDTH, tm)
        tmp_ref[...] = _silu_of_half(mm(OFF_ZA + c0)) * conv_a
        act_ref[:, ACT_YA + c0:ACT_YA + c0 + NBLK] = (mm(OFF_BG + c0) * tmp_ref[...]).astype(BF16)
    for c0 in range(0, D_SSM, NBLK):
        act_ref[:, ACT_ZS + c0:ACT_ZS + c0 + NBLK] = _silu_of_half(mm(OFF_ZS + c0)).astype(BF16)
    def finish_xbc(c0, buf):
        conv_s = _conv_taps(slab_ref, buf, csw_ref, csb_ref, c0, SSM_CONV_WIDTH, tm)
        xbc_ref[:, c0:c0 + NBLK] = _silu_of_half(conv_s).astype(BF16)

    for k, c0 in enumerate(range(0, D_XBC, NBLK)):
        _conv_fill(mm(OFF_XBC + c0), slab_ref, k % 2, hist_ref, hist_x0 + c0 // V7X_LANES, tm)
        if k > 0:
            finish_xbc(c0 - NBLK, (k - 1) % 2)
    finish_xbc(D_XBC - NBLK, (D_XBC // NBLK - 1) % 2)
    for c0 in range(0, 2 * D_MODEL, NBLK):
        act_ref[:, ACT_GA + c0:ACT_GA + c0 + NBLK] = _sigmoid_of_half(mm(OFF_GA + c0)).astype(BF16)


def _in_proj(x, w_main_all, layer, conv_a_w, conv_s_w, conv_s_b, ln_g=None, ln_b=None, *, tm, seq):
    t = x.shape[0]
    apply_ln = ln_g is not None
    row = lambda i: (i, 0)
    const = lambda i: (0, 0)

    def resident(a):
        return pl.BlockSpec(a.shape, const, pipeline_mode=pl.Buffered(1))

    in_specs = [pl.BlockSpec((tm, D_MODEL), row)]
    args = [x]
    if apply_ln:
        in_specs += [resident(ln_g), resident(ln_b)]
        args += [ln_g, ln_b]
    in_specs.append(pl.BlockSpec((None,) + w_main_all.shape[1:], lambda i: (layer, 0, 0),
                                 pipeline_mode=pl.Buffered(1)))
    params = [conv_a_w, conv_s_w, conv_s_b]
    in_specs += [resident(a) for a in params]
    args += [w_main_all] + params
    widths = [(N_ACT + WIDTH_PAD, BF16), (D_XBC + WIDTH_PAD, BF16), (DT_W, F32)]
    if apply_ln:
        widths.append((D_MODEL, F32))
    n_hist = (D_CONV_BR + D_XBC) // V7X_LANES
    return pl.pallas_call(
        functools.partial(_in_proj_kernel, apply_ln=apply_ln, tm=tm, tiles_per_seq=seq // tm),
        grid=(t // tm,),
        in_specs=in_specs,
        out_specs=[pl.BlockSpec((tm, w), row) for w, _ in widths],
        out_shape=[jax.ShapeDtypeStruct((t, w), d) for w, d in widths],
        scratch_shapes=[
            pltpu.VMEM((tm, D_MODEL), BF16),
            pltpu.VMEM((2, NBLK // V7X_LANES, HALO + tm, V7X_LANES), F32),
            pltpu.VMEM((n_hist, HALO, V7X_LANES), F32),
            pltpu.VMEM((tm, NBLK), F32),
        ],
        compiler_params=pltpu.CompilerParams(
            dimension_semantics=("arbitrary",),
            vmem_limit_bytes=V7X_VMEM_LIMIT_BYTES),
        name="in_proj_ln" if apply_ln else "in_proj",
    )(*args)


def _split3_by_lane_group(v, lane):
    hi = v.astype(BF16)
    r1 = v - hi.astype(F32)
    mid = r1.astype(BF16)
    lo = (r1 - mid.astype(F32)).astype(BF16)
    return jnp.where(lane < SSM_HEADS, hi, jnp.where(lane < 2 * SSM_HEADS, mid, lo))


def _ssd_steps(xbc_ref, dt_ref, dtb_ref, alog_ref, dskip_ref, expand_ref, tril_ref,
               state_ref, ys_ref, slot, tq):
    q = SSM_CHUNK
    lane = lax.broadcasted_iota(jnp.int32, (q, DT_W), 1)
    causal = (lax.broadcasted_iota(jnp.int32, (q, q), 0)
              >= lax.broadcasted_iota(jnp.int32, (q, q), 1))
    lane_pair = lax.broadcasted_iota(jnp.int32, (q, 2 * SSM_HEAD_DIM), 1)
    steps = []
    for r0 in range(0, tq, q):
        rows = slice(r0, r0 + q)
        ck = {}

        def prologue(rows=rows, ck=ck):
            a_neg = -jnp.exp(alog_ref[...])
            xdt = dt_ref[rows, :] + dtb_ref[...]
            dt = jnp.maximum(xdt, 0.0) + jnp.log1p(jnp.exp(-jnp.abs(xdt)))
            a = dt * a_neg
            a_hi = a.astype(BF16)
            a_r1 = a - a_hi.astype(F32)
            a_mid = a_r1.astype(BF16)
            a_lo = (a_r1 - a_mid.astype(F32)).astype(BF16)
            cs = jnp.dot(tril_ref[...], jnp.concatenate([a_hi, a_mid, a_lo], axis=1),
                         preferred_element_type=F32)
            a_cum = cs[:, 0:DT_W] + cs[:, DT_W:2 * DT_W] + cs[:, 2 * DT_W:3 * DT_W]
            a2 = a_cum * LOG2E
            ea = jnp.exp2(a2)
            wst = dt * jnp.exp2(a2[q - 1:q, :] - a2)
            pieces = jnp.concatenate([_split3_by_lane_group(wst, lane),
                                      _split3_by_lane_group(ea, lane)], axis=0)
            expd = jnp.dot(pieces, expand_ref[:, 0:D_SSM], preferred_element_type=F32)
            ck["a2"] = a2
            ck["wst_x"] = expd[0:q, :]
            ck["ea_x"] = expd[q:2 * q, :]
            ck["row2_t"] = (a2 - jnp.log(dt) * LOG2E).T
            for g in range(SSM_GROUPS):
                b_g = xbc_ref[rows, D_SSM + g * SSM_STATE:D_SSM + (g + 1) * SSM_STATE]
                c_g = xbc_ref[rows, D_SSM + D_BC + g * SSM_STATE:D_SSM + D_BC + (g + 1) * SSM_STATE]
                ck["cb", g] = lax.dot_general(c_g, b_g, (((1,), (1,)), ((), ())),
                                              preferred_element_type=F32)
                ck["y_off", g] = jnp.dot(c_g, state_ref[g].astype(BF16), preferred_element_type=F32)

        steps.append(prologue)
        for g in range(SSM_GROUPS):
            def group(g=g, rows=rows, ck=ck):
                a2, wst_x, ea_x, row2_t = ck["a2"], ck["wst_x"], ck["ea_x"], ck["row2_t"]
                gcols = slice(g * GROUP_W, (g + 1) * GROUP_W)
                b_g = xbc_ref[rows, D_SSM + g * SSM_STATE:D_SSM + (g + 1) * SSM_STATE]
                cb = ck["cb", g]
                st = state_ref[g]
                x_gb = xbc_ref[rows, gcols]
                x_g = x_gb.astype(F32)
                y_off = ck["y_off", g] * ea_x[:, gcols]
                for j in range(HEADS_PER_GROUP // 2):
                    l_pair = []
                    for hh in range(2):
                        hd = g * HEADS_PER_GROUP + 2 * j + hh
                        seg = a2[:, hd:hd + 1] - row2_t[hd:hd + 1, :]
                        dec = jnp.exp2(jnp.where(causal, seg, -jnp.inf))
                        l_pair.append((cb * dec).astype(BF16))
                    pc = slice(j * 2 * SSM_HEAD_DIM, (j + 1) * 2 * SSM_HEAD_DIM)
                    x_pb = x_gb[:, pc]
                    zero = jnp.zeros_like(x_pb)
                    w_p = jnp.concatenate([jnp.where(lane_pair < SSM_HEAD_DIM, x_pb, zero),
                                           jnp.where(lane_pair >= SSM_HEAD_DIM, x_pb, zero)], axis=0)
                    y_diag = jnp.dot(jnp.concatenate(l_pair, axis=1), w_p, preferred_element_type=F32)
                    ac = slice(g * GROUP_W + j * 2 * SSM_HEAD_DIM,
                               g * GROUP_W + (j + 1) * 2 * SSM_HEAD_DIM)
                    ys_ref[slot, rows, ac] = y_diag + y_off[:, pc] + x_g[:, pc] * dskip_ref[:, ac]
                xw = (x_g * wst_x[:, gcols]).astype(BF16)
                new = lax.dot_general(b_g, xw, (((0,), (0,)), ((), ())), preferred_element_type=F32)
                state_ref[g] = st * ea_x[q - 1:q, gcols] + new

            steps.append(group)
    return steps


def _proj_steps(act_ref, h_ref, ng_ref, wa_ref, ws_ref, wo_ref, lng_ref, lnb_ref, out_ref,
                ys_ref, slot, ya_ref, ysp_ref, hn_ref, mixed_ref):
    n_blk = D_MODEL // KBLK
    steps = []
    for nb in range(n_blk):
        def branch_a(nb=nb):
            cols = slice(nb * KBLK, (nb + 1) * KBLK)
            ya_ref[:, cols] = jnp.dot(act_ref[:, ACT_YA:ACT_YA + D_CONV_BR], wa_ref[:, cols],
                                      preferred_element_type=F32)
        steps.append(branch_a)
    for g in range(SSM_GROUPS):
        def norm_s(g=g):
            gcols = slice(g * GROUP_W, (g + 1) * GROUP_W)
            hg = ys_ref[slot, :, gcols] * act_ref[:, ACT_ZS + g * GROUP_W:ACT_ZS + (g + 1) * GROUP_W].astype(F32)
            ms = jnp.mean(hg * hg, axis=-1, keepdims=True)
            hn_ref[:, gcols] = (hg * lax.rsqrt(ms + RMS_EPS) * ng_ref[:, gcols]).astype(BF16)
        steps.append(norm_s)
    for nb in range(n_blk):
        def branch_s(nb=nb):
            cols = slice(nb * KBLK, (nb + 1) * KBLK)
            ysp_ref[:, cols] = jnp.dot(hn_ref[:, 0:D_SSM], ws_ref[:, cols], preferred_element_type=F32)
        steps.append(branch_s)
    for nb in range(n_blk):
        def merge(nb=nb):
            cols = slice(nb * KBLK, (nb + 1) * KBLK)
            sig_a = act_ref[:, ACT_GA + nb * KBLK:ACT_GA + (nb + 1) * KBLK].astype(F32)
            sig_s = act_ref[:, ACT_GS + nb * KBLK:ACT_GS + (nb + 1) * KBLK].astype(F32)
            mixed_ref[:, cols] = (sig_a * ya_ref[:, cols] + sig_s * ysp_ref[:, cols]).astype(BF16)
        steps.append(merge)
    for nb in range(n_blk):
        def project(nb=nb):
            cols = slice(nb * KBLK, (nb + 1) * KBLK)
            out_ref[:, cols] = jnp.dot(mixed_ref[:, 0:D_MODEL], wo_ref[:, cols], preferred_element_type=F32)
        steps.append(project)

    def deepnorm():
        out_ref[...] = _layer_norm(DEEPNORM_ALPHA * h_ref[...] + out_ref[...], lng_ref[...], lnb_ref[...])

    steps.append(deepnorm)
    return steps


def _interleave(a, b):
    if len(a) < len(b):
        a, b = b, a
    done = 0
    for i, fa in enumerate(a):
        fa()
        due = ((i + 1) * len(b)) // len(a)
        for fb in b[done:due]:
            fb()
        done = due


def _mixer_kernel(xbc_ref, dt_ref, act_ref, h_ref, dtb_ref, alog_ref,
                  dskip_ref, ng_ref, wa_ref, ws_ref, wo_ref, lng_ref, lnb_ref,
                  expand_ref, tril_ref,
                  out_ref,
                  state_ref, ys_ref, ya_ref, ysp_ref, hn_ref, mixed_ref, *, tq, n_tiles):
    c = pl.program_id(1)
    slot = c % 2

    def ssd():
        return _ssd_steps(xbc_ref, dt_ref, dtb_ref, alog_ref, dskip_ref, expand_ref, tril_ref,
                          state_ref, ys_ref, slot, tq)

    def proj():
        return _proj_steps(act_ref, h_ref, ng_ref, wa_ref, ws_ref, wo_ref, lng_ref, lnb_ref, out_ref,
                           ys_ref, 1 - slot, ya_ref, ysp_ref, hn_ref, mixed_ref)

    @pl.when(c == 0)
    def _():
        state_ref[...] = jnp.zeros_like(state_ref)
        for f in ssd():
            f()

    @pl.when(jnp.logical_and(c > 0, c < n_tiles))
    def _():
        _interleave(ssd(), proj())

    @pl.when(c == n_tiles)
    def _():
        for f in proj():
            f()


def _mixer(xbc, dt, act, h, p, layer, *, bsz, seq, tq):
    t = bsz * seq
    nt = seq // tq
    cur = lambda b, c: (b * nt + jnp.minimum(c, nt - 1), 0)
    prev = lambda b, c: (b * nt + jnp.maximum(c - 1, 0), 0)
    const2 = lambda b, c: (0, 0)

    def full(a):
        return pl.BlockSpec(a.shape, const2, pipeline_mode=pl.Buffered(1))

    def of_layer(a):
        return pl.BlockSpec((None,) + a.shape[1:], lambda b, c: (layer, 0, 0), pipeline_mode=pl.Buffered(1))

    params = [p["dt_bias"], p["a_log"], p["d_skip"],
              p["norm_s_g"], p["w_a_out"], p["w_s_out"], p["w_o"], p["ln_g"], p["ln_b"],
              p["expand"], p["tril"]]
    stacked = ("w_a_out", "w_s_out", "w_o")
    param_specs = [of_layer(p[k]) if k in stacked else full(p[k])
                   for k in ("dt_bias", "a_log", "d_skip", "norm_s_g", "w_a_out", "w_s_out", "w_o",
                             "ln_g", "ln_b", "expand", "tril")]
    return pl.pallas_call(
        functools.partial(_mixer_kernel, tq=tq, n_tiles=nt),
        grid=(bsz, nt + 1),
        in_specs=[pl.BlockSpec((tq, D_XBC + WIDTH_PAD), cur), pl.BlockSpec((tq, DT_W), cur),
                  pl.BlockSpec((tq, N_ACT + WIDTH_PAD), prev), pl.BlockSpec((tq, D_MODEL), prev)]
                 + param_specs,
        out_specs=pl.BlockSpec((tq, D_MODEL), prev),
        out_shape=jax.ShapeDtypeStruct((t, D_MODEL), F32),
        scratch_shapes=[
            pltpu.VMEM((SSM_GROUPS, SSM_STATE, GROUP_W), F32),
            pltpu.VMEM((2, tq, D_SSM), F32),
            pltpu.VMEM((tq, D_MODEL), F32),
            pltpu.VMEM((tq, D_MODEL), F32),
            pltpu.VMEM((tq, D_SSM + WIDTH_PAD), BF16),
            pltpu.VMEM((tq, D_MODEL + WIDTH_PAD), BF16),
        ],
        compiler_params=pltpu.CompilerParams(
            dimension_semantics=("arbitrary", "arbitrary"),
            vmem_limit_bytes=V7X_VMEM_LIMIT_BYTES),
        name="mixer",
    )(xbc, dt, act, h, *params)


def _pad_heads(v):
    return jnp.pad(jnp.tile(v, DT_REP), (0, DT_W - DT_REP * SSM_HEADS)).reshape(1, DT_W)


def _pad_cols(w):
    return jnp.pad(w, ((0, 0),) * (w.ndim - 1) + ((0, WIDTH_PAD),))


def kernel(x, ln_in_g, ln_in_b, w_in, conv_a_w, w_a_out, conv_s_w, conv_s_b, dt_bias, a_log, d_skip,
           norm_s_g, w_s_out, w_o, ln_g, ln_b):
    bsz, seq, _ = x.shape
    t = bsz * seq
    tq = 2 * SSM_CHUNK
    tm = min(512, seq)

    head_of_row = jnp.arange(DT_W) % SSM_HEADS
    valid_row = jnp.arange(DT_W) < DT_REP * SSM_HEADS
    head_of_col = jnp.arange(D_SSM) // SSM_HEAD_DIM
    expand = _pad_cols(((head_of_row[:, None] == head_of_col[None, :]) & valid_row[:, None]).astype(BF16))
    tril = (jnp.arange(SSM_CHUNK)[:, None] >= jnp.arange(SSM_CHUNK)[None, :]).astype(BF16)

    col = jnp.arange(N_MAIN + DT_W)
    halved = ((col >= OFF_ZA) & (col < OFF_XBC)) | ((col >= OFF_GA) & (col < N_MAIN))
    col_scale = jnp.where(halved, 0.5, 1.0).astype(F32)[None, :]

    w_dt = jnp.pad(jnp.tile(w_in[:, :, DT_COL_IN_W:DT_COL_IN_W + SSM_HEADS], (1, 1, DT_REP)),
                   ((0, 0), (0, 0), (0, DT_W - DT_REP * SSM_HEADS)))
    w_main_all = jnp.concatenate([w_in[:, :, :DT_COL_IN_W], w_in[:, :, DT_COL_IN_W + SSM_HEADS:], w_dt],
                                 axis=2)
    w_main_all = (w_main_all * col_scale).astype(BF16)
    wa_all = _pad_cols(w_a_out.astype(BF16))
    ws_all = _pad_cols(w_s_out.astype(BF16))
    wo_all = _pad_cols(w_o.astype(BF16))

    h = x.reshape(t, D_MODEL)
    for i in range(DEPTH):
        conv_params = (conv_a_w[i], 0.5 * conv_s_w[i], 0.5 * conv_s_b[i].reshape(1, -1))
        if i == 0:
            act, xbc, dt, h = _in_proj(h, w_main_all, i, *conv_params, ln_in_g.reshape(1, -1),
                                       ln_in_b.reshape(1, -1), tm=tm, seq=seq)
        else:
            act, xbc, dt = _in_proj(h, w_main_all, i, *conv_params, tm=tm, seq=seq)
        p = {
            "dt_bias": _pad_heads(dt_bias[i]), "a_log": _pad_heads(a_log[i]),
            "d_skip": jnp.repeat(d_skip[i], SSM_HEAD_DIM).reshape(1, -1),
            "norm_s_g": norm_s_g[i].reshape(1, -1),
            "w_a_out": wa_all, "w_s_out": ws_all, "w_o": wo_all,
            "ln_g": ln_g[i].reshape(1, -1), "ln_b": ln_b[i].reshape(1, -1),
            "expand": expand, "tril": tril,
        }
        h = _mixer(xbc, dt, act, h, p, i, bsz=bsz, seq=seq, tq=tq)
    return h.reshape(bsz, seq, D_MODEL)
```

```python
import functools

import jax
import jax.numpy as jnp
from jax import lax
from jax.experimental import pallas as pl
from jax.experimental.pallas import tpu as pltpu

D_MODEL = 1024
DEPTH = 2
D_CONV_BR = D_MODEL
CONV_A_WIDTH = 3
D_SSM = 2 * D_MODEL
SSM_HEAD_DIM = 64
SSM_HEADS = D_SSM // SSM_HEAD_DIM
SSM_GROUPS = 4
HEADS_PER_GROUP = SSM_HEADS // SSM_GROUPS
SSM_STATE = 128
SSM_CONV_WIDTH = 4
SSM_CHUNK = 128
D_BC = SSM_GROUPS * SSM_STATE
D_XBC = D_SSM + 2 * D_BC
GROUP_W = HEADS_PER_GROUP * SSM_HEAD_DIM
DEEPNORM_ALPHA = (2 * DEPTH) ** 0.25
LN_EPS = 1e-5
RMS_EPS = 1e-5
LOG2E = 1.4426950408889634

OFF_U = 0
OFF_BG = OFF_U + D_CONV_BR
OFF_CG = OFF_BG + D_CONV_BR
OFF_ZA = OFF_CG + D_CONV_BR
OFF_ZS = OFF_ZA + D_CONV_BR
OFF_XBC = OFF_ZS + D_SSM
OFF_GA = OFF_XBC + D_XBC
OFF_GS = OFF_GA + D_MODEL
N_MAIN = OFF_GS + D_MODEL
DT_COL_IN_W = OFF_GA

ACT_YA = 0
ACT_ZS = ACT_YA + D_CONV_BR
ACT_GA = ACT_ZS + D_SSM
ACT_GS = ACT_GA + D_MODEL
N_ACT = ACT_GS + D_MODEL

V7X_LANES = 128
V7X_SUBLANES = 8
V7X_VMEM_LIMIT_BYTES = 56 * 1024 * 1024

WIDTH_PAD = V7X_LANES
DT_W = V7X_LANES
DT_REP = 3
NBLK = 512
KBLK = 256
N_SLABS = 3
HALO = V7X_SUBLANES

F32 = jnp.float32
BF16 = jnp.bfloat16


def _layer_norm(x, g, b):
    mu = jnp.mean(x, axis=-1, keepdims=True)
    xc = x - mu
    var = jnp.mean(xc * xc, axis=-1, keepdims=True)
    return xc * lax.rsqrt(var + LN_EPS) * g + b


def _sigmoid_of_half(hx):
    return 0.5 * jnp.tanh(hx) + 0.5


def _silu_of_half(hx):
    return hx * (1.0 + jnp.tanh(hx))


def _conv_fill(v, slab_ref, buf, hist_ref, hist0, tm):
    for t in range(NBLK // V7X_LANES):
        slab_ref[buf, t, 0:HALO, :] = hist_ref[hist0 + t]
        slab_ref[buf, t, HALO:HALO + tm, :] = v[:, t * V7X_LANES:(t + 1) * V7X_LANES]
        hist_ref[hist0 + t] = slab_ref[buf, t, tm:tm + HALO, :]


def _conv_taps(slab_ref, buf, w_ref, b_ref, col0, width, tm):
    outs = []
    for t in range(NBLK // V7X_LANES):
        cols = slice(col0 + t * V7X_LANES, col0 + (t + 1) * V7X_LANES)
        acc = None
        for k in range(width):
            sh = width - 1 - k
            term = w_ref[k:k + 1, cols] * slab_ref[buf, t, HALO - sh:HALO - sh + tm, :]
            acc = term if acc is None else acc + term
        if b_ref is not None:
            acc = acc + b_ref[:, cols]
        outs.append(acc)
    return jnp.concatenate(outs, axis=1)


def _in_proj_kernel(*refs, apply_ln, tm, tiles_per_seq):
    if apply_ln:
        (x_ref, g_ref, b_ref, w_ref, caw_ref, csw_ref, csb_ref,
         act_ref, xbc_ref, dt_ref, h_ref, hb_ref, slab_ref, hist_ref, tmp_ref) = refs
    else:
        (x_ref, w_ref, caw_ref, csw_ref, csb_ref,
         act_ref, xbc_ref, dt_ref, hb_ref, slab_ref, hist_ref, tmp_ref) = refs

    @pl.when(pl.program_id(0) % tiles_per_seq == 0)
    def _():
        hist_ref[...] = jnp.zeros_like(hist_ref)

    x = x_ref[...]
    if apply_ln:
        x = _layer_norm(x, g_ref[...], b_ref[...])
        h_ref[...] = x
    hb_ref[...] = x.astype(BF16)
    dt_ref[...] = jnp.dot(hb_ref[...], w_ref[:, N_MAIN:N_MAIN + DT_W], preferred_element_type=F32)
    act_ref[:, N_ACT:N_ACT + WIDTH_PAD] = jnp.zeros((tm, WIDTH_PAD), BF16)
    xbc_ref[:, D_XBC:D_XBC + WIDTH_PAD] = jnp.zeros((tm, WIDTH_PAD), BF16)

    def mm(c0):
        return jnp.dot(hb_ref[...], w_ref[:, c0:c0 + NBLK], preferred_element_type=F32)

    hist_x0 = D_CONV_BR // V7X_LANES
    for c0 in range(0, D_CONV_BR, NBLK):
        tmp_ref[...] = mm(OFF_U + c0)
        _conv_fill(mm(OFF_CG + c0) * tmp_ref[...], slab_ref, 0, hist_ref, c0 // V7X_LANES, tm)
        conv_a = _conv_taps(slab_ref, 0, caw_ref, None, c0, CONV_A_WIDTH, tm)
        tmp_ref[...] = _silu_of_half(mm(OFF_ZA + c0)) * conv_a
        act_ref[:, ACT_YA + c0:ACT_YA + c0 + NBLK] = (mm(OFF_BG + c0) * tmp_ref[...]).astype(BF16)
    for c0 in range(0, D_SSM, NBLK):
        act_ref[:, ACT_ZS + c0:ACT_ZS + c0 + NBLK] = _silu_of_half(mm(OFF_ZS + c0)).astype(BF16)
    def finish_xbc(k):
        c0 = k * NBLK
        conv_s = _conv_taps(slab_ref, k % N_SLABS, csw_ref, csb_ref, c0, SSM_CONV_WIDTH, tm)
        xbc_ref[:, c0:c0 + NBLK] = _silu_of_half(conv_s).astype(BF16)

    n_xbc = D_XBC // NBLK
    for k in range(n_xbc):
        _conv_fill(mm(OFF_XBC + k * NBLK), slab_ref, k % N_SLABS, hist_ref,
                   hist_x0 + k * NBLK // V7X_LANES, tm)
        if k >= N_SLABS - 1:
            finish_xbc(k - (N_SLABS - 1))
    for k in range(n_xbc - (N_SLABS - 1), n_xbc):
        finish_xbc(k)
    for c0 in range(0, 2 * D_MODEL, NBLK):
        act_ref[:, ACT_GA + c0:ACT_GA + c0 + NBLK] = _sigmoid_of_half(mm(OFF_GA + c0)).astype(BF16)


def _in_proj(x, w_main_all, layer, conv_a_w, conv_s_w, conv_s_b, ln_g=None, ln_b=None, *, tm, seq):
    t = x.shape[0]
    apply_ln = ln_g is not None
    row = lambda i: (i, 0)
    const = lambda i: (0, 0)

    def resident(a):
        return pl.BlockSpec(a.shape, const, pipeline_mode=pl.Buffered(1))

    in_specs = [pl.BlockSpec((tm, D_MODEL), row)]
    args = [x]
    if apply_ln:
        in_specs += [resident(ln_g), resident(ln_b)]
        args += [ln_g, ln_b]
    in_specs.append(pl.BlockSpec((None,) + w_main_all.shape[1:], lambda i: (layer, 0, 0),
                                 pipeline_mode=pl.Buffered(1)))
    params = [conv_a_w, conv_s_w, conv_s_b]
    in_specs += [resident(a) for a in params]
    args += [w_main_all] + params
    widths = [(N_ACT + WIDTH_PAD, BF16), (D_XBC + WIDTH_PAD, BF16), (DT_W, F32)]
    if apply_ln:
        widths.append((D_MODEL, F32))
    n_hist = (D_CONV_BR + D_XBC) // V7X_LANES
    return pl.pallas_call(
        functools.partial(_in_proj_kernel, apply_ln=apply_ln, tm=tm, tiles_per_seq=seq // tm),
        grid=(t // tm,),
        in_specs=in_specs,
        out_specs=[pl.BlockSpec((tm, w), row) for w, _ in widths],
        out_shape=[jax.ShapeDtypeStruct((t, w), d) for w, d in widths],
        scratch_shapes=[
            pltpu.VMEM((tm, D_MODEL), BF16),
            pltpu.VMEM((N_SLABS, NBLK // V7X_LANES, HALO + tm, V7X_LANES), F32),
            pltpu.VMEM((n_hist, HALO, V7X_LANES), F32),
            pltpu.VMEM((tm, NBLK), F32),
        ],
        compiler_params=pltpu.CompilerParams(
            dimension_semantics=("arbitrary",),
            vmem_limit_bytes=V7X_VMEM_LIMIT_BYTES),
        name="in_proj_ln" if apply_ln else "in_proj",
    )(*args)


def _split3_by_lane_group(v, lane):
    hi = v.astype(BF16)
    r1 = v - hi.astype(F32)
    mid = r1.astype(BF16)
    lo = (r1 - mid.astype(F32)).astype(BF16)
    return jnp.where(lane < SSM_HEADS, hi, jnp.where(lane < 2 * SSM_HEADS, mid, lo))


def _ssd_steps(xbc_ref, dt_ref, dtb_ref, alog_ref, dskip_ref, expand_ref, tril_ref,
               state_ref, ys_ref, slot, tq):
    q = SSM_CHUNK
    lane = lax.broadcasted_iota(jnp.int32, (q, DT_W), 1)
    causal = (lax.broadcasted_iota(jnp.int32, (q, q), 0)
              >= lax.broadcasted_iota(jnp.int32, (q, q), 1))
    lane_pair = lax.broadcasted_iota(jnp.int32, (q, 2 * SSM_HEAD_DIM), 1)
    steps = []
    for r0 in range(0, tq, q):
        rows = slice(r0, r0 + q)
        ck = {}

        def prologue(rows=rows, ck=ck):
            a_neg = -jnp.exp(alog_ref[...])
            xdt = dt_ref[rows, :] + dtb_ref[...]
            dt = jnp.maximum(xdt, 0.0) + jnp.log1p(jnp.exp(-jnp.abs(xdt)))
            a = dt * a_neg
            a_hi = a.astype(BF16)
            a_r1 = a - a_hi.astype(F32)
            a_mid = a_r1.astype(BF16)
            a_lo = (a_r1 - a_mid.astype(F32)).astype(BF16)
            cs = jnp.dot(tril_ref[...], jnp.concatenate([a_hi, a_mid, a_lo], axis=1),
                         preferred_element_type=F32)
            a_cum = cs[:, 0:DT_W] + cs[:, DT_W:2 * DT_W] + cs[:, 2 * DT_W:3 * DT_W]
            a2 = a_cum * LOG2E
            ea = jnp.exp2(a2)
            wst = dt * jnp.exp2(a2[q - 1:q, :] - a2)
            pieces = jnp.concatenate([_split3_by_lane_group(wst, lane),
                                      _split3_by_lane_group(ea, lane)], axis=0)
            expd = jnp.dot(pieces, expand_ref[:, 0:D_SSM], preferred_element_type=F32)
            ck["a2"] = a2
            ck["wst_x"] = expd[0:q, :]
            ck["ea_x"] = expd[q:2 * q, :]
            ck["row2_t"] = (a2 - jnp.log(dt) * LOG2E).T
            for g in range(SSM_GROUPS):
                b_g = xbc_ref[rows, D_SSM + g * SSM_STATE:D_SSM + (g + 1) * SSM_STATE]
                c_g = xbc_ref[rows, D_SSM + D_BC + g * SSM_STATE:D_SSM + D_BC + (g + 1) * SSM_STATE]
                ck["cb", g] = lax.dot_general(c_g, b_g, (((1,), (1,)), ((), ())),
                                              preferred_element_type=F32)
                ck["y_off", g] = jnp.dot(c_g, state_ref[g].astype(BF16), preferred_element_type=F32)

        steps.append(prologue)
        for g in range(SSM_GROUPS):
            def group(g=g, rows=rows, ck=ck):
                a2, wst_x, ea_x, row2_t = ck["a2"], ck["wst_x"], ck["ea_x"], ck["row2_t"]
                gcols = slice(g * GROUP_W, (g + 1) * GROUP_W)
                b_g = xbc_ref[rows, D_SSM + g * SSM_STATE:D_SSM + (g + 1) * SSM_STATE]
                cb = ck["cb", g]
                st = state_ref[g]
                x_gb = xbc_ref[rows, gcols]
                x_g = x_gb.astype(F32)
                y_off = ck["y_off", g] * ea_x[:, gcols]
                for j in range(HEADS_PER_GROUP // 2):
                    l_pair = []
                    for hh in range(2):
                        hd = g * HEADS_PER_GROUP + 2 * j + hh
                        seg = a2[:, hd:hd + 1] - row2_t[hd:hd + 1, :]
                        dec = jnp.exp2(jnp.where(causal, seg, -jnp.inf))
                        l_pair.append((cb * dec).astype(BF16))
                    pc = slice(j * 2 * SSM_HEAD_DIM, (j + 1) * 2 * SSM_HEAD_DIM)
                    x_pb = x_gb[:, pc]
                    zero = jnp.zeros_like(x_pb)
                    w_p = jnp.concatenate([jnp.where(lane_pair < SSM_HEAD_DIM, x_pb, zero),
                                           jnp.where(lane_pair >= SSM_HEAD_DIM, x_pb, zero)], axis=0)
                    y_diag = jnp.dot(jnp.concatenate(l_pair, axis=1), w_p, preferred_element_type=F32)
                    ac = slice(g * GROUP_W + j * 2 * SSM_HEAD_DIM,
                               g * GROUP_W + (j + 1) * 2 * SSM_HEAD_DIM)
                    ys_ref[slot, rows, ac] = y_diag + y_off[:, pc] + x_g[:, pc] * dskip_ref[:, ac]
                xw = (x_g * wst_x[:, gcols]).astype(BF16)
                new = lax.dot_general(b_g, xw, (((0,), (0,)), ((), ())), preferred_element_type=F32)
                state_ref[g] = st * ea_x[q - 1:q, gcols] + new

            steps.append(group)
    return steps


def _proj_steps(act_ref, h_ref, ng_ref, wa_ref, ws_ref, wo_ref, lng_ref, lnb_ref, out_ref,
                ys_ref, slot, ya_ref, ysp_ref, hn_ref, mixed_ref):
    n_blk = D_MODEL // KBLK
    steps = []
    for nb in range(n_blk):
        def branch_a(nb=nb):
            cols = slice(nb * KBLK, (nb + 1) * KBLK)
            ya_ref[:, cols] = jnp.dot(act_ref[:, ACT_YA:ACT_YA + D_CONV_BR], wa_ref[:, cols],
                                      preferred_element_type=F32)
        steps.append(branch_a)
    for g in range(SSM_GROUPS):
        def norm_s(g=g):
            gcols = slice(g * GROUP_W, (g + 1) * GROUP_W)
            hg = ys_ref[slot, :, gcols] * act_ref[:, ACT_ZS + g * GROUP_W:ACT_ZS + (g + 1) * GROUP_W].astype(F32)
            ms = jnp.mean(hg * hg, axis=-1, keepdims=True)
            hn_ref[:, gcols] = (hg * lax.rsqrt(ms + RMS_EPS) * ng_ref[:, gcols]).astype(BF16)
        steps.append(norm_s)
    for nb in range(n_blk):
        def branch_s(nb=nb):
            cols = slice(nb * KBLK, (nb + 1) * KBLK)
            ysp_ref[:, cols] = jnp.dot(hn_ref[:, 0:D_SSM], ws_ref[:, cols], preferred_element_type=F32)
        steps.append(branch_s)
    for nb in range(n_blk):
        def merge(nb=nb):
            cols = slice(nb * KBLK, (nb + 1) * KBLK)
            sig_a = act_ref[:, ACT_GA + nb * KBLK:ACT_GA + (nb + 1) * KBLK].astype(F32)
            sig_s = act_ref[:, ACT_GS + nb * KBLK:ACT_GS + (nb + 1) * KBLK].astype(F32)
            mixed_ref[:, cols] = (sig_a * ya_ref[:, cols] + sig_s * ysp_ref[:, cols]).astype(BF16)
        steps.append(merge)
    for nb in range(n_blk):
        def project(nb=nb):
            cols = slice(nb * KBLK, (nb + 1) * KBLK)
            out_ref[:, cols] = jnp.dot(mixed_ref[:, 0:D_MODEL], wo_ref[:, cols], preferred_element_type=F32)
        steps.append(project)

    def deepnorm():
        out_ref[...] = _layer_norm(DEEPNORM_ALPHA * h_ref[...] + out_ref[...], lng_ref[...], lnb_ref[...])

    steps.append(deepnorm)
    return steps


def _interleave(a, b):
    if len(a) < len(b):
        a, b = b, a
    done = 0
    for i, fa in enumerate(a):
        fa()
        due = ((i + 1) * len(b)) // len(a)
        for fb in b[done:due]:
            fb()
        done = due


def _mixer_kernel(xbc_ref, dt_ref, act_ref, h_ref, dtb_ref, alog_ref,
                  dskip_ref, ng_ref, wa_ref, ws_ref, wo_ref, lng_ref, lnb_ref,
                  expand_ref, tril_ref,
                  out_ref,
                  state_ref, ys_ref, ya_ref, ysp_ref, hn_ref, mixed_ref, *, tq, n_tiles):
    c = pl.program_id(1)
    slot = c % 2

    def ssd():
        return _ssd_steps(xbc_ref, dt_ref, dtb_ref, alog_ref, dskip_ref, expand_ref, tril_ref,
                          state_ref, ys_ref, slot, tq)

    def proj():
        return _proj_steps(act_ref, h_ref, ng_ref, wa_ref, ws_ref, wo_ref, lng_ref, lnb_ref, out_ref,
                           ys_ref, 1 - slot, ya_ref, ysp_ref, hn_ref, mixed_ref)

    @pl.when(c == 0)
    def _():
        state_ref[...] = jnp.zeros_like(state_ref)
        for f in ssd():
            f()

    @pl.when(jnp.logical_and(c > 0, c < n_tiles))
    def _():
        _interleave(ssd(), proj())

    @pl.when(c == n_tiles)
    def _():
        for f in proj():
            f()


def _mixer(xbc, dt, act, h, p, layer, *, bsz, seq, tq):
    t = bsz * seq
    nt = seq // tq
    cur = lambda b, c: (b * nt + jnp.minimum(c, nt - 1), 0)
    prev = lambda b, c: (b * nt + jnp.maximum(c - 1, 0), 0)
    const2 = lambda b, c: (0, 0)

    def full(a):
        return pl.BlockSpec(a.shape, const2, pipeline_mode=pl.Buffered(1))

    def of_layer(a):
        return pl.BlockSpec((None,) + a.shape[1:], lambda b, c: (layer, 0, 0), pipeline_mode=pl.Buffered(1))

    params = [p["dt_bias"], p["a_log"], p["d_skip"],
              p["norm_s_g"], p["w_a_out"], p["w_s_out"], p["w_o"], p["ln_g"], p["ln_b"],
              p["expand"], p["tril"]]
    stacked = ("w_a_out", "w_s_out", "w_o")
    param_specs = [of_layer(p[k]) if k in stacked else full(p[k])
                   for k in ("dt_bias", "a_log", "d_skip", "norm_s_g", "w_a_out", "w_s_out", "w_o",
                             "ln_g", "ln_b", "expand", "tril")]
    return pl.pallas_call(
        functools.partial(_mixer_kernel, tq=tq, n_tiles=nt),
        grid=(bsz, nt + 1),
        in_specs=[pl.BlockSpec((tq, D_XBC + WIDTH_PAD), cur), pl.BlockSpec((tq, DT_W), cur),
                  pl.BlockSpec((tq, N_ACT + WIDTH_PAD), prev), pl.BlockSpec((tq, D_MODEL), prev)]
                 + param_specs,
        out_specs=pl.BlockSpec((tq, D_MODEL), prev),
        out_shape=jax.ShapeDtypeStruct((t, D_MODEL), F32),
        scratch_shapes=[
            pltpu.VMEM((SSM_GROUPS, SSM_STATE, GROUP_W), F32),
            pltpu.VMEM((2, tq, D_SSM), F32),
            pltpu.VMEM((tq, D_MODEL), F32),
            pltpu.VMEM((tq, D_MODEL), F32),
            pltpu.VMEM((tq, D_SSM + WIDTH_PAD), BF16),
            pltpu.VMEM((tq, D_MODEL + WIDTH_PAD), BF16),
        ],
        compiler_params=pltpu.CompilerParams(
            dimension_semantics=("arbitrary", "arbitrary"),
            vmem_limit_bytes=V7X_VMEM_LIMIT_BYTES),
        name="mixer",
    )(xbc, dt, act, h, *params)


def _pad_heads(v):
    return jnp.pad(jnp.tile(v, DT_REP), (0, DT_W - DT_REP * SSM_HEADS)).reshape(1, DT_W)


def _pad_cols(w):
    return jnp.pad(w, ((0, 0),) * (w.ndim - 1) + ((0, WIDTH_PAD),))


def kernel(x, ln_in_g, ln_in_b, w_in, conv_a_w, w_a_out, conv_s_w, conv_s_b, dt_bias, a_log, d_skip,
           norm_s_g, w_s_out, w_o, ln_g, ln_b):
    bsz, seq, _ = x.shape
    t = bsz * seq
    tq = 2 * SSM_CHUNK
    tm = min(512, seq)

    head_of_row = jnp.arange(DT_W) % SSM_HEADS
    valid_row = jnp.arange(DT_W) < DT_REP * SSM_HEADS
    head_of_col = jnp.arange(D_SSM) // SSM_HEAD_DIM
    expand = _pad_cols(((head_of_row[:, None] == head_of_col[None, :]) & valid_row[:, None]).astype(BF16))
    tril = (jnp.arange(SSM_CHUNK)[:, None] >= jnp.arange(SSM_CHUNK)[None, :]).astype(BF16)

    col = jnp.arange(N_MAIN + DT_W)
    halved = ((col >= OFF_ZA) & (col < OFF_XBC)) | ((col >= OFF_GA) & (col < N_MAIN))
    col_scale = jnp.where(halved, 0.5, 1.0).astype(F32)[None, :]

    w_dt = jnp.pad(jnp.tile(w_in[:, :, DT_COL_IN_W:DT_COL_IN_W + SSM_HEADS], (1, 1, DT_REP)),
                   ((0, 0), (0, 0), (0, DT_W - DT_REP * SSM_HEADS)))
    w_main_all = jnp.concatenate([w_in[:, :, :DT_COL_IN_W], w_in[:, :, DT_COL_IN_W + SSM_HEADS:], w_dt],
                                 axis=2)
    w_main_all = (w_main_all * col_scale).astype(BF16)
    wa_all = _pad_cols(w_a_out.astype(BF16))
    ws_all = _pad_cols(w_s_out.astype(BF16))
    wo_all = _pad_cols(w_o.astype(BF16))

    h = x.reshape(t, D_MODEL)
    for i in range(DEPTH):
        conv_params = (conv_a_w[i], 0.5 * conv_s_w[i], 0.5 * conv_s_b[i].reshape(1, -1))
        if i == 0:
            act, xbc, dt, h = _in_proj(h, w_main_all, i, *conv_params, ln_in_g.reshape(1, -1),
                                       ln_in_b.reshape(1, -1), tm=tm, seq=seq)
        else:
            act, xbc, dt = _in_proj(h, w_main_all, i, *conv_params, tm=tm, seq=seq)
        p = {
            "dt_bias": _pad_heads(dt_bias[i]), "a_log": _pad_heads(a_log[i]),
            "d_skip": jnp.repeat(d_skip[i], SSM_HEAD_DIM).reshape(1, -1),
            "norm_s_g": norm_s_g[i].reshape(1, -1),
            "w_a_out": wa_all, "w_s_out": ws_all, "w_o": wo_all,
            "ln_g": ln_g[i].reshape(1, -1), "ln_b": ln_b[i].reshape(1, -1),
            "expand": expand, "tril": tril,
        }
        h = _mixer(xbc, dt, act, h, p, i, bsz=bsz, seq=seq, tq=tq)
    return h.reshape(bsz, seq, D_MODEL)
```
